```python
import jax, jax.numpy as jnp
from jax import lax
import numpy as np

D_MODEL = 1024
BATCH = 8
SEQ = 4096
DEPTH = 2

CHUNK = 64
Q_BLOCK = 128
ROPE_THETA = 500000.0
NORM_EPS = 1e-6
N_MIXERS = 2

DA_HEADS = 8
DA_HEAD_DIM = D_MODEL // (2 * DA_HEADS)
DA_ROT = DA_HEAD_DIM // 4

MLA_HEADS = 16
MLA_Q_RANK = 384
MLA_KV_RANK = 256
MLA_NOPE = 64
MLA_ROPE = 32
MLA_V = 64

FFN_HIDDEN = 2816
CONV_WIDTH = 3

N_DA_LAYERS = (DEPTH + N_MIXERS - 1) // N_MIXERS
N_MLA_LAYERS = DEPTH // N_MIXERS

kernel_name = "hybrid_diffattn_mla_convffn"


def rms_norm(x, g):
    xf = x.astype(jnp.float32)
    y = xf * lax.rsqrt(jnp.mean(xf * xf, axis=-1, keepdims=True) + NORM_EPS)
    return (y * g.astype(jnp.float32)).astype(x.dtype)


def rope_tables(seq_len, rot_dim):
    inv = ROPE_THETA ** (-jnp.arange(0, rot_dim, 2, dtype=jnp.float32) / rot_dim)
    ang = jnp.arange(seq_len, dtype=jnp.float32)[:, None] * inv[None, :]
    return jnp.cos(ang), jnp.sin(ang)


def apply_rope(x, cos, sin):
    shape = (x.shape[1],) + (1,) * (x.ndim - 3) + (cos.shape[-1],)
    c = cos.reshape(shape).astype(x.dtype)
    s = sin.reshape(shape).astype(x.dtype)
    x1, x2 = jnp.split(x, 2, axis=-1)
    return jnp.concatenate([x1 * c - x2 * s, x2 * c + x1 * s], axis=-1)


def chunk_mask(blk, seq_len):
    q_pos = blk * Q_BLOCK + jnp.arange(Q_BLOCK)
    k_pos = jnp.arange(seq_len)
    return (k_pos[None, :] // CHUNK) <= (q_pos[:, None] // CHUNK)


def to_blocks(t):
    b, s = t.shape[0], t.shape[1]
    return jnp.moveaxis(t.reshape((b, s // Q_BLOCK, Q_BLOCK) + t.shape[2:]), 1, 0)


def from_blocks(t):
    t = jnp.moveaxis(t, 0, 1)
    return t.reshape((t.shape[0], t.shape[1] * t.shape[2]) + t.shape[3:])


def masked_softmax(s, mask):
    s = jnp.where(mask, s, jnp.finfo(jnp.float32).min)
    return jax.nn.softmax(s, axis=-1)


def diff_attention(h, w_qkv, lq1, lk1, lq2, lk2, subln_g, w_o, lambda_init, cos, sin):
    b, s, _ = h.shape
    qkv = h @ w_qkv
    q, k, v = jnp.split(qkv, 3, axis=-1)
    q = q.reshape(b, s, DA_HEADS, 2, DA_HEAD_DIM)
    k = k.reshape(b, s, DA_HEADS, 2, DA_HEAD_DIM)
    v = v.reshape(b, s, DA_HEADS, 2 * DA_HEAD_DIM)
    q = jnp.concatenate([apply_rope(q[..., :DA_ROT], cos, sin), q[..., DA_ROT:]], axis=-1)
    k = jnp.concatenate([apply_rope(k[..., :DA_ROT], cos, sin), k[..., DA_ROT:]], axis=-1)
    lam = (jnp.exp(jnp.sum(lq1.astype(jnp.float32) * lk1.astype(jnp.float32)))
           - jnp.exp(jnp.sum(lq2.astype(jnp.float32) * lk2.astype(jnp.float32)))
           + lambda_init)
    scale = DA_HEAD_DIM ** -0.5

    def block(args):
        qb, blk = args
        sc = jnp.einsum('bqhcd,bkhcd->bhcqk', qb, k).astype(jnp.float32) * scale
        p = masked_softmax(sc, chunk_mask(blk, s)[None, None, None])
        a = p[:, :, 0] - lam * p[:, :, 1]
        return jnp.einsum('bhqk,bkhe->bqhe', a.astype(v.dtype), v)

    nb = s // Q_BLOCK
    o = from_blocks(lax.map(block, (to_blocks(q), jnp.arange(nb))))
    o = rms_norm(o, subln_g) * (1.0 - lambda_init)
    return o.reshape(b, s, D_MODEL) @ w_o


def mla_attention(h, w_in, q_norm_g, kv_norm_g, w_uq, w_ukv, w_o, cos, sin):
    b, s, _ = h.shape
    proj = h @ w_in
    c_q = rms_norm(proj[..., :MLA_Q_RANK], q_norm_g)
    c_kv = rms_norm(proj[..., MLA_Q_RANK:MLA_Q_RANK + MLA_KV_RANK], kv_norm_g)
    k_pe = apply_rope(proj[..., MLA_Q_RANK + MLA_KV_RANK:], cos, sin)
    q = (c_q @ w_uq).reshape(b, s, MLA_HEADS, MLA_NOPE + MLA_ROPE)
    q_nope = q[..., :MLA_NOPE]
    q_pe = apply_rope(q[..., MLA_NOPE:], cos, sin)
    kv = (c_kv @ w_ukv).reshape(b, s, MLA_HEADS, MLA_NOPE + MLA_V)
    k_nope = kv[..., :MLA_NOPE]
    v = kv[..., MLA_NOPE:]
    scale = (MLA_NOPE + MLA_ROPE) ** -0.5

    def block(args):
        qn, qp, blk = args
        sc = (jnp.einsum('bqhd,bkhd->bhqk', qn, k_nope)
              + jnp.einsum('bqhr,bkr->bhqk', qp, k_pe)).astype(jnp.float32) * scale
        p = masked_softmax(sc, chunk_mask(blk, s)[None, None])
        return jnp.einsum('bhqk,bkhe->bqhe', p.astype(v.dtype), v)

    nb = s // Q_BLOCK
    o = from_blocks(lax.map(block, (to_blocks(q_nope), to_blocks(q_pe), jnp.arange(nb))))
    return o.reshape(b, s, MLA_HEADS * MLA_V) @ w_o


def conv_ffn(h, w_up, conv_w, conv_b, w_down):
    u = h @ w_up
    c = lax.conv_general_dilated(
        u, conv_w[:, None, :].astype(u.dtype), window_strides=(1,),
        padding=[(CONV_WIDTH - 1, 0)], dimension_numbers=('NWC', 'WIO', 'NWC'),
        feature_group_count=u.shape[-1]) + conv_b
    g, val = jnp.split(c, 2, axis=-1)
    return (jax.nn.silu(g) * val) @ w_down


def setup_inputs(seed: int = 0) -> dict:
    key = jax.random.key(seed)
    ks = jax.random.split(key, 24)
    f32 = jnp.float32

    def nrm(k, shape, scale):
        return jax.random.normal(k, shape, f32) * scale

    def gain(k, shape):
        return 1.0 + 0.01 * jax.random.normal(k, shape, f32)

    mla_in = MLA_Q_RANK + MLA_KV_RANK + MLA_ROPE
    return {
        "x": jax.random.normal(ks[0], (BATCH, SEQ, D_MODEL), f32),
        "attn_norm_g": gain(ks[1], (DEPTH, D_MODEL)),
        "ffn_norm_g": gain(ks[2], (DEPTH, D_MODEL)),
        "da_w_qkv": nrm(ks[3], (N_DA_LAYERS, D_MODEL, 3 * D_MODEL), D_MODEL ** -0.5),
        "da_lam_q1": nrm(ks[4], (N_DA_LAYERS, DA_HEAD_DIM), 0.1),
        "da_lam_k1": nrm(ks[5], (N_DA_LAYERS, DA_HEAD_DIM), 0.1),
        "da_lam_q2": nrm(ks[6], (N_DA_LAYERS, DA_HEAD_DIM), 0.1),
        "da_lam_k2": nrm(ks[7], (N_DA_LAYERS, DA_HEAD_DIM), 0.1),
        "da_subln_g": gain(ks[8], (N_DA_LAYERS, 2 * DA_HEAD_DIM)),
        "da_w_o": nrm(ks[9], (N_DA_LAYERS, D_MODEL, D_MODEL), D_MODEL ** -0.5),
        "mla_w_in": nrm(ks[10], (N_MLA_LAYERS, D_MODEL, mla_in), D_MODEL ** -0.5),
        "mla_q_norm_g": gain(ks[11], (N_MLA_LAYERS, MLA_Q_RANK)),
        "mla_kv_norm_g": gain(ks[12], (N_MLA_LAYERS, MLA_KV_RANK)),
        "mla_w_uq": nrm(ks[13], (N_MLA_LAYERS, MLA_Q_RANK, MLA_HEADS * (MLA_NOPE + MLA_ROPE)), MLA_Q_RANK ** -0.5),
        "mla_w_ukv": nrm(ks[14], (N_MLA_LAYERS, MLA_KV_RANK, MLA_HEADS * (MLA_NOPE + MLA_V)), MLA_KV_RANK ** -0.5),
        "mla_w_o": nrm(ks[15], (N_MLA_LAYERS, MLA_HEADS * MLA_V, D_MODEL), (MLA_HEADS * MLA_V) ** -0.5),
        "ffn_w_up": nrm(ks[16], (DEPTH, D_MODEL, 2 * FFN_HIDDEN), D_MODEL ** -0.5),
        "ffn_conv_w": nrm(ks[17], (DEPTH, CONV_WIDTH, 2 * FFN_HIDDEN), CONV_WIDTH ** -0.5),
        "ffn_conv_b": nrm(ks[18], (DEPTH, 2 * FFN_HIDDEN), 0.01),
        "ffn_w_down": nrm(ks[19], (DEPTH, FFN_HIDDEN, D_MODEL), FFN_HIDDEN ** -0.5),
        "final_norm_g": gain(ks[20], (D_MODEL,)),
    }


def reference(x, attn_norm_g, ffn_norm_g, da_w_qkv, da_lam_q1, da_lam_k1, da_lam_q2, da_lam_k2,
              da_subln_g, da_w_o, mla_w_in, mla_q_norm_g, mla_kv_norm_g, mla_w_uq, mla_w_ukv,
              mla_w_o, ffn_w_up, ffn_conv_w, ffn_conv_b, ffn_w_down, final_norm_g):
    s = x.shape[1]
    cos_da, sin_da = rope_tables(s, DA_ROT)
    cos_mla, sin_mla = rope_tables(s, MLA_ROPE)
    for i in range(DEPTH):
        h = rms_norm(x, attn_norm_g[i])
        j = i // N_MIXERS
        if i % N_MIXERS == 0:
            lambda_init = 0.8 - 0.6 * float(np.exp(-0.3 * i))
            x = x + diff_attention(h, da_w_qkv[j], da_lam_q1[j], da_lam_k1[j], da_lam_q2[j],
                                   da_lam_k2[j], da_subln_g[j], da_w_o[j], lambda_init,
                                   cos_da, sin_da)
        else:
            x = x + mla_attention(h, mla_w_in[j], mla_q_norm_g[j], mla_kv_norm_g[j],
                                  mla_w_uq[j], mla_w_ukv[j], mla_w_o[j], cos_mla, sin_mla)
        h = rms_norm(x, ffn_norm_g[i])
        x = x + conv_ffn(h, ffn_w_up[i], ffn_conv_w[i], ffn_conv_b[i], ffn_w_down[i])
    return rms_norm(x, final_norm_g)
```

```python
import functools

import numpy as np
import jax
import jax.numpy as jnp
from jax import lax
from jax.experimental import pallas as pl
from jax.experimental.pallas import tpu as pltpu

D_MODEL = 1024
DEPTH = 2
CHUNK = 64
ROPE_THETA = 500000.0
NORM_EPS = 1e-6
N_MIXERS = 2

DA_HEADS = 8
DA_HEAD_DIM = D_MODEL // (2 * DA_HEADS)
DA_ROT = DA_HEAD_DIM // 4

MLA_HEADS = 16
MLA_Q_RANK = 384
MLA_KV_RANK = 256
MLA_NOPE = 64
MLA_ROPE = 32
MLA_V = 64

FFN_HIDDEN = 2816
CONV_WIDTH = 3

LANES = 128
BF16_SUBLANES = 16
MASK_VALUE = -1e30

ROW_TILE = 512
ATTN_TILE = 256
FFN_CHUNK = 256
HALO = BF16_SUBLANES

VMEM_LIMIT = 56 * 1024 * 1024


def _compiler_params(semantics):
    return pltpu.CompilerParams(dimension_semantics=semantics,
                                vmem_limit_bytes=VMEM_LIMIT)


def _rms(x, g):
    return x * lax.rsqrt(jnp.mean(x * x, axis=-1, keepdims=True) + NORM_EPS) * g


def _dot(a, b):
    return jnp.dot(a, b, preferred_element_type=jnp.float32)


def _dot_nt(a, b):
    return lax.dot_general(a, b, (((1,), (1,)), ((), ())),
                           preferred_element_type=jnp.float32)


def _rope_block(x, c, a, bm, half):
    return (x * c + pltpu.roll(x, half, axis=1) * a
            + pltpu.roll(x, LANES - half, axis=1) * bm)


def _rope_tables(seq_len, rot_dim, lane_start, period):
    half = rot_dim // 2
    inv = ROPE_THETA ** (-jnp.arange(0, rot_dim, 2, dtype=jnp.float32) / rot_dim)
    ang = jnp.arange(seq_len, dtype=jnp.float32)[:, None] * inv[None, :]
    cos, sin = jnp.cos(ang), jnp.sin(ang)
    r = np.arange(LANES) % period - lane_start
    first = (r >= 0) & (r < half)
    second = (r >= half) & (r < rot_dim)
    idx = np.where(first, r, np.where(second, r - half, 0))
    cos_l, sin_l = cos[:, idx], sin[:, idx]
    c = jnp.where(first | second, cos_l, 1.0)
    a = jnp.where(second, sin_l, 0.0)
    bm = jnp.where(first, -sin_l, 0.0)
    return c, a, bm


def _da_proj_kernel(x_ref, g_ref, w_ref, c_ref, a_ref, bm_ref, o_ref):
    h = _rms(x_ref[...], g_ref[...]).astype(jnp.bfloat16)
    c, a, bm = c_ref[...], a_ref[...], bm_ref[...]
    scale = DA_HEAD_DIM ** -0.5
    for part in range(3):
        lo = part * D_MODEL
        y = _dot(h, w_ref[:, lo:lo + D_MODEL])
        for blk in range(D_MODEL // LANES):
            yb = y[:, blk * LANES:(blk + 1) * LANES]
            if part < 2:
                yb = _rope_block(yb, c, a, bm, DA_ROT // 2)
            if part == 0:
                yb = yb * scale
            o_ref[:, lo + blk * LANES:lo + (blk + 1) * LANES] = yb.astype(o_ref.dtype)


def _da_proj(x2d, g, w_qkv, tables, seq_len):
    t = x2d.shape[0]
    tm = ROW_TILE
    pos_blocks = seq_len // tm
    tab_spec = pl.BlockSpec((tm, LANES), lambda i: (i % pos_blocks, 0))
    return pl.pallas_call(
        _da_proj_kernel,
        grid=(t // tm,),
        in_specs=[
            pl.BlockSpec((tm, D_MODEL), lambda i: (i, 0)),
            pl.BlockSpec((1, D_MODEL), lambda i: (0, 0)),
            pl.BlockSpec((D_MODEL, 3 * D_MODEL), lambda i: (0, 0)),
            tab_spec, tab_spec, tab_spec,
        ],
        out_specs=pl.BlockSpec((tm, 3 * D_MODEL), lambda i: (i, 0)),
        out_shape=jax.ShapeDtypeStruct((t, 3 * D_MODEL), jnp.bfloat16),
        compiler_params=_compiler_params(("parallel",)),
        name="da_proj",
    )(x2d, g, w_qkv, *tables)


def _chunk_mask(tq, tk):
    qc = lax.broadcasted_iota(jnp.int32, (tq, tk), 0) // CHUNK
    kc = lax.broadcasted_iota(jnp.int32, (tq, tk), 1) // CHUNK
    return kc <= qc


def _tri_tables(nq):
    qi = np.repeat(np.arange(nq), np.arange(1, nq + 1))
    ki = np.concatenate([np.arange(n + 1) for n in range(nq)])
    return jnp.asarray(qi, jnp.int32), jnp.asarray(ki, jnp.int32)


def _da_attn_kernel(qt_ref, kt_ref, lq1_ref, lk1_ref, lq2_ref, lk2_ref, sg_ref,
                    q_ref, k_ref, v_ref, o_ref, qz_ref, m_ref, l_ref, acc_ref,
                    *, lambda_init):
    step = pl.program_id(2)
    qi, ki = qt_ref[step], kt_ref[step]
    tq, tk = q_ref.shape[0], k_ref.shape[0]

    @pl.when(ki == 0)
    def _init():
        q = q_ref[...]
        lane = lax.broadcasted_iota(jnp.int32, q.shape, 1)
        zero = jnp.zeros_like(q)
        qz_ref[0] = jnp.where(lane < DA_HEAD_DIM, q, zero)
        qz_ref[1] = jnp.where(lane >= DA_HEAD_DIM, q, zero)
        m_ref[...] = jnp.full(m_ref.shape, MASK_VALUE, jnp.float32)
        l_ref[...] = jnp.zeros(l_ref.shape, jnp.float32)
        acc_ref[...] = jnp.zeros(acc_ref.shape, jnp.float32)

    def _step(masked):
        k = k_ref[...]
        v = v_ref[...]
        for comp in range(2):
            s = _dot_nt(qz_ref[comp], k)
            if masked:
                s = jnp.where(_chunk_mask(tq, tk), s, MASK_VALUE)
            m_prev = m_ref[comp]
            m_new = jnp.maximum(m_prev, jnp.max(s, axis=1, keepdims=True))
            alpha = jnp.exp(m_prev - m_new)
            p = jnp.exp(s - m_new)
            l_ref[comp] = alpha * l_ref[comp] + jnp.sum(p, axis=1, keepdims=True)
            acc_ref[comp] = alpha * acc_ref[comp] + _dot(p.astype(v.dtype), v)
            m_ref[comp] = m_new

    @pl.when(ki < qi)
    def _full():
        _step(False)

    @pl.when(ki == qi)
    def _diag():
        _step(True)
        lam = (jnp.exp(jnp.sum(lq1_ref[...] * lk1_ref[...], keepdims=True))
               - jnp.exp(jnp.sum(lq2_ref[...] * lk2_ref[...], keepdims=True))
               + lambda_init)
        o = acc_ref[0] * (1.0 / l_ref[0]) - lam * (acc_ref[1] * (1.0 / l_ref[1]))
        o = _rms(o, sg_ref[...]) * (1.0 - lambda_init)
        o_ref[...] = o.astype(o_ref.dtype)


def _da_attention(qkv, lq1, lk1, lq2, lk2, subln_g, lambda_init):
    b, s, _ = qkv.shape
    tq = tk = ATTN_TILE
    nq = s // tq
    hd = 2 * DA_HEAD_DIM
    qt, kt = _tri_tables(nq)
    vec = pl.BlockSpec((1, DA_HEAD_DIM), lambda bi, h, st, qt, kt: (0, 0))
    grid_spec = pltpu.PrefetchScalarGridSpec(
        num_scalar_prefetch=2,
        grid=(b, DA_HEADS, qt.shape[0]),
        in_specs=[
            vec, vec, vec, vec,
            pl.BlockSpec((1, hd), lambda bi, h, st, qt, kt: (0, 0)),
            pl.BlockSpec((None, tq, hd), lambda bi, h, st, qt, kt: (bi, qt[st], h)),
            pl.BlockSpec((None, tk, hd),
                         lambda bi, h, st, qt, kt: (bi, kt[st], DA_HEADS + h)),
            pl.BlockSpec((None, tk, hd),
                         lambda bi, h, st, qt, kt: (bi, kt[st], 2 * DA_HEADS + h)),
        ],
        out_specs=pl.BlockSpec((None, tq, hd), lambda bi, h, st, qt, kt: (bi, qt[st], h)),
        scratch_shapes=[
            pltpu.VMEM((2, tq, hd), jnp.bfloat16),
            pltpu.VMEM((2, tq, 1), jnp.float32),
            pltpu.VMEM((2, tq, 1), jnp.float32),
            pltpu.VMEM((2, tq, hd), jnp.float32),
        ],
    )
    return pl.pallas_call(
        functools.partial(_da_attn_kernel, lambda_init=lambda_init),
        grid_spec=grid_spec,
        out_shape=jax.ShapeDtypeStruct((b, s, DA_HEADS * hd), jnp.bfloat16),
        compiler_params=_compiler_params(("parallel", "parallel", "arbitrary")),
        name="da_attn",
    )(qt, kt, lq1, lk1, lq2, lk2, subln_g, qkv, qkv, qkv)


def _mla_proj_kernel(x_ref, g_ref, win_ref, gq_ref, gkv_ref, wuq_ref, wuk_ref, wuv_ref,
                     cq_ref, aq_ref, bq_ref, ck_ref, ak_ref, bk_ref,
                     q_ref, k_ref, v_ref):
    h = _rms(x_ref[...], g_ref[...]).astype(jnp.bfloat16)
    proj = _dot(h, win_ref[...])
    kv_lo = MLA_Q_RANK
    pe_lo = MLA_Q_RANK + MLA_KV_RANK
    c_q = _rms(proj[:, :kv_lo], gq_ref[...]).astype(jnp.bfloat16)
    c_kv = _rms(proj[:, kv_lo:pe_lo], gkv_ref[...]).astype(jnp.bfloat16)
    half = MLA_ROPE // 2
    k_pe = _rope_block(proj[:, pe_lo:pe_lo + LANES], ck_ref[...], ak_ref[...], bk_ref[...], half)
    cq, aq, bq = cq_ref[...], aq_ref[...], bq_ref[...]
    q = _dot(c_q, wuq_ref[...])
    k = _dot(c_kv, wuk_ref[...])
    for hd in range(MLA_HEADS):
        sl = slice(hd * LANES, (hd + 1) * LANES)
        q_ref[:, sl] = _rope_block(q[:, sl], cq, aq, bq, half).astype(q_ref.dtype)
        k_ref[:, sl] = (k[:, sl] + k_pe).astype(k_ref.dtype)
    v_ref[...] = _dot(c_kv, wuv_ref[...]).astype(v_ref.dtype)


def _mla_proj(x2d, g, w_in, g_q, g_kv, w_uq, w_uk, w_uv, q_tables, k_tables, seq_len):
    t = x2d.shape[0]
    tm = ROW_TILE
    pos_blocks = seq_len // tm
    tab_spec = pl.BlockSpec((tm, LANES), lambda i: (i % pos_blocks, 0))

    def whole(arr):
        return pl.BlockSpec(arr.shape, lambda i: (0, 0))

    qk_cols = MLA_HEADS * LANES
    v_cols = MLA_HEADS * MLA_V
    return pl.pallas_call(
        _mla_proj_kernel,
        grid=(t // tm,),
        in_specs=[pl.BlockSpec((tm, D_MODEL), lambda i: (i, 0)),
                  whole(g), whole(w_in), whole(g_q), whole(g_kv),
                  whole(w_uq), whole(w_uk), whole(w_uv)] + [tab_spec] * 6,
        out_specs=[pl.BlockSpec((tm, qk_cols), lambda i: (i, 0)),
                   pl.BlockSpec((tm, qk_cols), lambda i: (i, 0)),
                   pl.BlockSpec((tm, v_cols), lambda i: (i, 0))],
        out_shape=[jax.ShapeDtypeStruct((t, qk_cols), jnp.bfloat16),
                   jax.ShapeDtypeStruct((t, qk_cols), jnp.bfloat16),
                   jax.ShapeDtypeStruct((t, v_cols), jnp.bfloat16)],
        compiler_params=_compiler_params(("parallel",)),
        name="mla_proj",
    )(x2d, g, w_in, g_q, g_kv, w_uq, w_uk, w_uv, *q_tables, *k_tables)


def _mla_attn_kernel(qt_ref, kt_ref, q_ref, k_ref, v_ref, o_ref, m_ref, l_ref, acc_ref):
    step = pl.program_id(2)
    qi, ki = qt_ref[step], kt_ref[step]
    tq, tk = q_ref.shape[0], k_ref.shape[0]

    @pl.when(ki == 0)
    def _init():
        m_ref[...] = jnp.full(m_ref.shape, MASK_VALUE, jnp.float32)
        l_ref[...] = jnp.zeros(l_ref.shape, jnp.float32)
        acc_ref[...] = jnp.zeros(acc_ref.shape, jnp.float32)

    def _lane_pick(first, second):
        lane = lax.broadcasted_iota(jnp.int32, (tq, LANES), 1)
        return jnp.where(lane < MLA_V, first, second)

    def _step(masked):
        v = v_ref[...]
        vlane = lax.broadcasted_iota(jnp.int32, v.shape, 1)
        vzero = jnp.zeros_like(v)
        v_half = (jnp.where(vlane < MLA_V, v, vzero), jnp.where(vlane >= MLA_V, v, vzero))
        alphas, pv = [], None
        for e in range(2):
            sl = slice(e * LANES, (e + 1) * LANES)
            s = _dot_nt(q_ref[:, sl], k_ref[:, sl])
            if masked:
                s = jnp.where(_chunk_mask(tq, tk), s, MASK_VALUE)
            m_prev = m_ref[e]
            m_new = jnp.maximum(m_prev, jnp.max(s, axis=1, keepdims=True))
            alpha = jnp.exp(m_prev - m_new)
            p = jnp.exp(s - m_new)
            l_ref[e] = alpha * l_ref[e] + jnp.sum(p, axis=1, keepdims=True)
            m_ref[e] = m_new
            alphas.append(alpha)
            contrib = _dot(p.astype(v.dtype), v_half[e])
            pv = contrib if pv is None else pv + contrib
        acc_ref[...] = _lane_pick(alphas[0], alphas[1]) * acc_ref[...] + pv

    @pl.when(ki < qi)
    def _full():
        _step(False)

    @pl.when(ki == qi)
    def _diag():
        _step(True)
        o = acc_ref[...] * _lane_pick(1.0 / l_ref[0], 1.0 / l_ref[1])
        o_ref[...] = o.astype(o_ref.dtype)


def _mla_attention(q, k, v):
    b, s, _ = q.shape
    tq = tk = ATTN_TILE
    nq = s // tq
    pairs = MLA_HEADS // 2
    qt, kt = _tri_tables(nq)
    grid_spec = pltpu.PrefetchScalarGridSpec(
        num_scalar_prefetch=2,
        grid=(b, pairs, qt.shape[0]),
        in_specs=[
            pl.BlockSpec((None, tq, 2 * LANES), lambda bi, hp, st, qt, kt: (bi, qt[st], hp)),
            pl.BlockSpec((None, tk, 2 * LANES), lambda bi, hp, st, qt, kt: (bi, kt[st], hp)),
            pl.BlockSpec((None, tk, LANES), lambda bi, hp, st, qt, kt: (bi, kt[st], hp)),
        ],
        out_specs=pl.BlockSpec((None, tq, LANES), lambda bi, hp, st, qt, kt: (bi, qt[st], hp)),
        scratch_shapes=[
            pltpu.VMEM((2, tq, 1), jnp.float32),
            pltpu.VMEM((2, tq, 1), jnp.float32),
            pltpu.VMEM((tq, LANES), jnp.float32),
        ],
    )
    return pl.pallas_call(
        _mla_attn_kernel,
        grid_spec=grid_spec,
        out_shape=jax.ShapeDtypeStruct((b, s, MLA_HEADS * MLA_V), jnp.bfloat16),
        compiler_params=_compiler_params(("parallel", "parallel", "arbitrary")),
        name="mla_attn",
    )(qt, kt, q, k, v)


def _out_proj_kernel(x_ref, o_ref, w_ref, y_ref):
    y_ref[...] = x_ref[...] + _dot(o_ref[...], w_ref[...])


def _out_proj(x2d, o2d, w_o):
    t = x2d.shape[0]
    tm = ROW_TILE
    return pl.pallas_call(
        _out_proj_kernel,
        grid=(t // tm,),
        in_specs=[pl.BlockSpec((tm, D_MODEL), lambda i: (i, 0)),
                  pl.BlockSpec((tm, D_MODEL), lambda i: (i, 0)),
                  pl.BlockSpec((D_MODEL, D_MODEL), lambda i: (0, 0))],
        out_specs=pl.BlockSpec((tm, D_MODEL), lambda i: (i, 0)),
        out_shape=jax.ShapeDtypeStruct((t, D_MODEL), jnp.float32),
        compiler_params=_compiler_params(("parallel",)),
        name="out_proj",
    )(x2d, o2d, w_o)


def _ffn_kernel(xh_ref, x_ref, g_ref, wup_ref, cw_ref, wdn_ref, gf_ref, y_ref,
                h_ref, acc_ref, *, pos_blocks, final_norm):
    tm = x_ref.shape[0]
    i = pl.program_id(0)
    g = g_ref[...]
    x = x_ref[...]
    halo = jnp.where(i % pos_blocks != 0, _rms(xh_ref[...], g), 0.0)
    h_ref[:HALO] = halo.astype(h_ref.dtype)
    h_ref[HALO:] = _rms(x, g).astype(h_ref.dtype)
    acc_ref[...] = jnp.zeros(acc_ref.shape, jnp.float32)

    def chunk(f, carry):
        u = _dot(h_ref[...], wup_ref[f])
        cw = cw_ref[f]
        c = (u[HALO:] * cw[2:3]
             + pltpu.roll(u, 1, axis=0)[HALO:] * cw[1:2]
             + pltpu.roll(u, 2, axis=0)[HALO:] * cw[0:1]
             + cw[3:4])
        gate, val = c[:, :FFN_CHUNK], c[:, FFN_CHUNK:]
        act = gate / (1.0 + jnp.exp(-gate)) * val
        acc_ref[...] += _dot(act.astype(jnp.bfloat16), wdn_ref[f])
        return carry

    lax.fori_loop(0, wup_ref.shape[0], chunk, 0)
    y = x + acc_ref[...]
    if final_norm:
        y = _rms(y, gf_ref[...])
    y_ref[...] = y


def _ffn(x2d, g, w_up, cw, w_down, g_final, seq_len, final_norm):
    t = x2d.shape[0]
    tm = ROW_TILE
    pos_blocks = seq_len // tm
    halo_per_tile = tm // HALO

    def whole(arr):
        nd = arr.ndim
        return pl.BlockSpec(arr.shape, lambda i: (0,) * nd)

    return pl.pallas_call(
        functools.partial(_ffn_kernel, pos_blocks=pos_blocks, final_norm=final_norm),
        grid=(t // tm,),
        in_specs=[
            pl.BlockSpec((HALO, D_MODEL),
                         lambda i: (jnp.maximum(i * halo_per_tile - 1, 0), 0)),
            pl.BlockSpec((tm, D_MODEL), lambda i: (i, 0)),
            whole(g), whole(w_up), whole(cw), whole(w_down), whole(g_final),
        ],
        out_specs=pl.BlockSpec((tm, D_MODEL), lambda i: (i, 0)),
        out_shape=jax.ShapeDtypeStruct((t, D_MODEL), jnp.float32),
        scratch_shapes=[
            pltpu.VMEM((HALO + tm, D_MODEL), jnp.bfloat16),
            pltpu.VMEM((tm, D_MODEL), jnp.float32),
        ],
        compiler_params=_compiler_params(("parallel",)),
        name="conv_ffn",
    )(x2d, x2d, g, w_up, cw, w_down, g_final)


def _ffn_weights(w_up, conv_w, conv_b, w_down):
    nf = FFN_HIDDEN // FFN_CHUNK

    def chunks(a):
        lead = a.shape[:-1]
        a = a.reshape(lead + (2, nf, FFN_CHUNK))
        a = jnp.moveaxis(a, -2, 0)
        return a.reshape((nf,) + lead + (2 * FFN_CHUNK,))

    w_up_c = chunks(w_up).astype(jnp.bfloat16)
    taps = chunks(conv_w)
    bias = chunks(conv_b[None])
    cw = jnp.concatenate([taps, bias, jnp.zeros((nf, 4, 2 * FFN_CHUNK), jnp.float32)], axis=1)
    w_dn_c = w_down.reshape(nf, FFN_CHUNK, D_MODEL).astype(jnp.bfloat16)
    return w_up_c, cw, w_dn_c


def kernel(x, attn_norm_g, ffn_norm_g, da_w_qkv, da_lam_q1, da_lam_k1, da_lam_q2, da_lam_k2,
           da_subln_g, da_w_o, mla_w_in, mla_q_norm_g, mla_kv_norm_g, mla_w_uq, mla_w_ukv,
           mla_w_o, ffn_w_up, ffn_conv_w, ffn_conv_b, ffn_w_down, final_norm_g):
    b, s, d = x.shape
    assert d == D_MODEL and s % ROW_TILE == 0 and s % ATTN_TILE == 0
    t = b * s
    bf16 = jnp.bfloat16
    x2d = x.reshape(t, d)
    g_final = final_norm_g.reshape(1, d)

    da_tables = _rope_tables(s, DA_ROT, 0, DA_HEAD_DIM)
    mla_k_tables = _rope_tables(s, MLA_ROPE, MLA_NOPE, LANES)
    mla_scale = (MLA_NOPE + MLA_ROPE) ** -0.5
    mla_q_tables = tuple(tb * mla_scale for tb in mla_k_tables)

    for i in range(DEPTH):
        j = i // N_MIXERS
        g_attn = attn_norm_g[i].reshape(1, d)
        if i % N_MIXERS == 0:
            lambda_init = 0.8 - 0.6 * float(np.exp(-0.3 * i))
            qkv = _da_proj(x2d, g_attn, da_w_qkv[j].astype(bf16), da_tables, s)
            o = _da_attention(qkv.reshape(b, s, 3 * d),
                              da_lam_q1[j].reshape(1, -1), da_lam_k1[j].reshape(1, -1),
                              da_lam_q2[j].reshape(1, -1), da_lam_k2[j].reshape(1, -1),
                              da_subln_g[j].reshape(1, -1), lambda_init)
            w_o = da_w_o[j]
        else:
            pe_pad = jnp.zeros((d, LANES), jnp.float32)
            pe_lo = MLA_Q_RANK + MLA_KV_RANK
            pe_pad = pe_pad.at[:, MLA_NOPE:MLA_NOPE + MLA_ROPE].set(mla_w_in[j][:, pe_lo:])
            w_in = jnp.concatenate([mla_w_in[j][:, :pe_lo], pe_pad], axis=1).astype(bf16)
            qk_dim = MLA_NOPE + MLA_ROPE
            w_uq = mla_w_uq[j].reshape(MLA_Q_RANK, MLA_HEADS, qk_dim)
            w_uq = jnp.pad(w_uq, ((0, 0), (0, 0), (0, LANES - qk_dim)))
            w_uq = w_uq.reshape(MLA_Q_RANK, MLA_HEADS * LANES).astype(bf16)
            w_ukv = mla_w_ukv[j].reshape(MLA_KV_RANK, MLA_HEADS, MLA_NOPE + MLA_V)
            w_uk = jnp.pad(w_ukv[:, :, :MLA_NOPE], ((0, 0), (0, 0), (0, LANES - MLA_NOPE)))
            w_uk = w_uk.reshape(MLA_KV_RANK, MLA_HEADS * LANES).astype(bf16)
            w_uv = w_ukv[:, :, MLA_NOPE:].reshape(MLA_KV_RANK, MLA_HEADS * MLA_V).astype(bf16)
            q, k, v = _mla_proj(x2d, g_attn, w_in, mla_q_norm_g[j].reshape(1, -1),
                                mla_kv_norm_g[j].reshape(1, -1), w_uq, w_uk, w_uv,
                                mla_q_tables, mla_k_tables, s)
            o = _mla_attention(q.reshape(b, s, -1), k.reshape(b, s, -1), v.reshape(b, s, -1))
            w_o = mla_w_o[j]
        x2d = _out_proj(x2d, o.reshape(t, d), w_o.astype(bf16))
        w_up_c, cw, w_dn_c = _ffn_weights(ffn_w_up[i], ffn_conv_w[i], ffn_conv_b[i],
                                          ffn_w_down[i])
        x2d = _ffn(x2d, ffn_norm_g[i].reshape(1, d), w_up_c, cw, w_dn_c, g_final, s,
                   final_norm=(i == DEPTH - 1))
    return x2d.reshape(b, s, d)
```

```python
import functools

import numpy as np
import jax
import jax.numpy as jnp
from jax import lax
from jax.experimental import pallas as pl
from jax.experimental.pallas import tpu as pltpu

D_MODEL = 1024
DEPTH = 2
CHUNK = 64
ROPE_THETA = 500000.0
NORM_EPS = 1e-6
N_MIXERS = 2

DA_HEADS = 8
DA_HEAD_DIM = D_MODEL // (2 * DA_HEADS)
DA_ROT = DA_HEAD_DIM // 4

MLA_HEADS = 16
MLA_Q_RANK = 384
MLA_KV_RANK = 256
MLA_NOPE = 64
MLA_ROPE = 32
MLA_V = 64

FFN_HIDDEN = 2816
CONV_WIDTH = 3

LANES = 128
BF16_SUBLANES = 16
MASK_VALUE = -1e30

ROW_TILE = 512
ATTN_TQ = 512
ATTN_TK = 256
CHUNK_SHIFT = CHUNK.bit_length() - 1
FFN_CHUNK = 256
HALO = BF16_SUBLANES

VMEM_LIMIT = 56 * 1024 * 1024


def _compiler_params(semantics):
    return pltpu.CompilerParams(dimension_semantics=semantics,
                                vmem_limit_bytes=VMEM_LIMIT)


def _rms(x, g):
    return x * lax.rsqrt(jnp.mean(x * x, axis=-1, keepdims=True) + NORM_EPS) * g


def _dot(a, b):
    return jnp.dot(a, b, preferred_element_type=jnp.float32)


def _rope_block(x, c, a, bm, half):
    return (x * c + pltpu.roll(x, half, axis=1) * a
            + pltpu.roll(x, LANES - half, axis=1) * bm)


def _rope_tables(seq_len, rot_dim, lane_start, period):
    half = rot_dim // 2
    inv = ROPE_THETA ** (-jnp.arange(0, rot_dim, 2, dtype=jnp.float32) / rot_dim)
    ang = jnp.arange(seq_len, dtype=jnp.float32)[:, None] * inv[None, :]
    cos, sin = jnp.cos(ang), jnp.sin(ang)
    r = np.arange(LANES) % period - lane_start
    first = (r >= 0) & (r < half)
    second = (r >= half) & (r < rot_dim)
    idx = np.where(first, r, np.where(second, r - half, 0))
    cos_l, sin_l = cos[:, idx], sin[:, idx]
    c = jnp.where(first | second, cos_l, 1.0)
    a = jnp.where(second, sin_l, 0.0)
    bm = jnp.where(first, -sin_l, 0.0)
    return c, a, bm


def _da_proj_kernel(x_ref, g_ref, w_ref, c_ref, a_ref, bm_ref, o_ref):
    h = _rms(x_ref[...], g_ref[...]).astype(jnp.bfloat16)
    c, a, bm = c_ref[...], a_ref[...], bm_ref[...]
    scale = DA_HEAD_DIM ** -0.5
    for part in range(3):
        lo = part * D_MODEL
        y = _dot(h, w_ref[:, lo:lo + D_MODEL])
        for blk in range(D_MODEL // LANES):
            yb = y[:, blk * LANES:(blk + 1) * LANES]
            if part < 2:
                yb = _rope_block(yb, c, a, bm, DA_ROT // 2)
            if part == 0:
                yb = yb * scale
            o_ref[:, lo + blk * LANES:lo + (blk + 1) * LANES] = yb.astype(o_ref.dtype)


def _da_proj(x2d, g, w_qkv, tables, seq_len):
    t = x2d.shape[0]
    tm = ROW_TILE
    pos_blocks = seq_len // tm
    tab_spec = pl.BlockSpec((tm, LANES), lambda i: (i % pos_blocks, 0))
    return pl.pallas_call(
        _da_proj_kernel,
        grid=(t // tm,),
        in_specs=[
            pl.BlockSpec((tm, D_MODEL), lambda i: (i, 0)),
            pl.BlockSpec((1, D_MODEL), lambda i: (0, 0)),
            pl.BlockSpec((D_MODEL, 3 * D_MODEL), lambda i: (0, 0)),
            tab_spec, tab_spec, tab_spec,
        ],
        out_specs=pl.BlockSpec((tm, 3 * D_MODEL), lambda i: (i, 0)),
        out_shape=jax.ShapeDtypeStruct((t, 3 * D_MODEL), jnp.bfloat16),
        compiler_params=_compiler_params(("parallel",)),
        name="da_proj",
    )(x2d, g, w_qkv, *tables)


def _attn_tile(k, wq, vt, m_ref, l_ref, acc_ref, mask):
    s = _dot(k, wq)
    if mask is not None:
        s = jnp.where(mask, s, MASK_VALUE)
    m_prev = m_ref[...]
    m_new = jnp.maximum(m_prev, jnp.max(s, axis=0, keepdims=True))
    alpha = jnp.exp(m_prev - m_new)
    p = jnp.exp(s - m_new)
    l_ref[...] = alpha * l_ref[...] + jnp.sum(p, axis=0, keepdims=True)
    m_ref[...] = m_new
    acc_ref[...] = alpha * acc_ref[...] + _dot(vt, p.astype(vt.dtype))


def _chunk_mask(k0, q0, tk, tq, n):
    kc = (k0 + lax.broadcasted_iota(jnp.int32, (tk, n), 0)) >> CHUNK_SHIFT
    qc = (q0 + (lax.broadcasted_iota(jnp.int32, (tk, n), 1) & (tq - 1))) >> CHUNK_SHIFT
    return kc <= qc


def _transpose_to_bf16(x):
    return x.astype(jnp.float32).T.astype(jnp.bfloat16)


def _fill_vt(v_ref, vt_ref):
    tk = vt_ref.shape[2]

    def body(ki, carry):
        k0 = pl.multiple_of(ki * tk, tk)
        vt_ref[ki] = _transpose_to_bf16(v_ref[pl.ds(k0, tk), :])
        return carry

    lax.fori_loop(0, vt_ref.shape[0], body, 0)


def _init_state(m_ref, l_ref, acc_ref):
    m_ref[...] = jnp.full(m_ref.shape, MASK_VALUE, jnp.float32)
    l_ref[...] = jnp.zeros(l_ref.shape, jnp.float32)
    acc_ref[...] = jnp.zeros(acc_ref.shape, jnp.float32)


def _da_attn_kernel(lq1_ref, lk1_ref, lq2_ref, lk2_ref, sg_ref, q_ref, k_ref, v_ref,
                    o_ref, vt_ref, wq_ref, m_ref, l_ref, acc_ref, *, lambda_init):
    tq, tk = ATTN_TQ, ATTN_TK
    n = 2 * tq
    _fill_vt(v_ref, vt_ref)
    lam = (jnp.exp(jnp.sum(lq1_ref[...] * lk1_ref[...], keepdims=True))
           - jnp.exp(jnp.sum(lq2_ref[...] * lk2_ref[...], keepdims=True))
           + lambda_init)

    def q_body(qi, carry):
        q0 = pl.multiple_of(qi * tq, tq)
        qt = q_ref[pl.ds(q0, tq), :].astype(jnp.float32).T
        row = lax.broadcasted_iota(jnp.int32, qt.shape, 0)
        wq_ref[:, :tq] = jnp.where(row < DA_HEAD_DIM, qt, 0.0).astype(wq_ref.dtype)
        wq_ref[:, tq:] = jnp.where(row >= DA_HEAD_DIM, qt, 0.0).astype(wq_ref.dtype)
        _init_state(m_ref, l_ref, acc_ref)

        def tile(ki, masked):
            k0 = pl.multiple_of(ki * tk, tk)
            mask = _chunk_mask(k0, q0, tk, tq, n) if masked else None
            _attn_tile(k_ref[pl.ds(k0, tk), :], wq_ref[...], vt_ref[ki],
                       m_ref, l_ref, acc_ref, mask)

        def kv_body(ki, c):
            tile(ki, False)
            return c

        first_diag = qi * (tq // tk)
        lax.fori_loop(0, first_diag, kv_body, 0)
        for d in range(tq // tk):
            tile(first_diag + d, True)

        o = acc_ref[...] * (1.0 / l_ref[...])
        o = (o[:, :tq] - lam * o[:, tq:]).T
        o = _rms(o, sg_ref[...]) * (1.0 - lambda_init)
        o_ref[pl.ds(q0, tq), :] = o.astype(o_ref.dtype)
        return carry

    lax.fori_loop(0, q_ref.shape[0] // tq, q_body, 0)


def _da_attention(qkv, lq1, lk1, lq2, lk2, subln_g, lambda_init):
    b, s, _ = qkv.shape
    tq, tk = ATTN_TQ, ATTN_TK
    hd = 2 * DA_HEAD_DIM
    vec = pl.BlockSpec((1, DA_HEAD_DIM), lambda bi, h: (0, 0))
    return pl.pallas_call(
        functools.partial(_da_attn_kernel, lambda_init=lambda_init),
        grid=(b, DA_HEADS),
        in_specs=[
            vec, vec, vec, vec,
            pl.BlockSpec((1, hd), lambda bi, h: (0, 0)),
            pl.BlockSpec((None, s, hd), lambda bi, h: (bi, 0, h)),
            pl.BlockSpec((None, s, hd), lambda bi, h: (bi, 0, DA_HEADS + h)),
            pl.BlockSpec((None, s, hd), lambda bi, h: (bi, 0, 2 * DA_HEADS + h)),
        ],
        out_specs=pl.BlockSpec((None, s, hd), lambda bi, h: (bi, 0, h)),
        out_shape=jax.ShapeDtypeStruct((b, s, DA_HEADS * hd), jnp.bfloat16),
        scratch_shapes=[
            pltpu.VMEM((s // tk, hd, tk), jnp.bfloat16),
            pltpu.VMEM((hd, 2 * tq), jnp.bfloat16),
            pltpu.VMEM((1, 2 * tq), jnp.float32),
            pltpu.VMEM((1, 2 * tq), jnp.float32),
            pltpu.VMEM((hd, 2 * tq), jnp.float32),
        ],
        compiler_params=_compiler_params(("parallel", "parallel")),
        name="da_attn",
    )(lq1, lk1, lq2, lk2, subln_g, qkv, qkv, qkv)


def _mla_proj_kernel(x_ref, g_ref, win_ref, gq_ref, gkv_ref, wuq_ref, wuk_ref, wuv_ref,
                     cq_ref, aq_ref, bq_ref, ck_ref, ak_ref, bk_ref,
                     q_ref, k_ref, v_ref):
    h = _rms(x_ref[...], g_ref[...]).astype(jnp.bfloat16)
    proj = _dot(h, win_ref[...])
    kv_lo = MLA_Q_RANK
    pe_lo = MLA_Q_RANK + MLA_KV_RANK
    c_q = _rms(proj[:, :kv_lo], gq_ref[...]).astype(jnp.bfloat16)
    c_kv = _rms(proj[:, kv_lo:pe_lo], gkv_ref[...]).astype(jnp.bfloat16)
    half = MLA_ROPE // 2
    k_pe = _rope_block(proj[:, pe_lo:pe_lo + LANES], ck_ref[...], ak_ref[...], bk_ref[...], half)
    cq, aq, bq = cq_ref[...], aq_ref[...], bq_ref[...]
    q = _dot(c_q, wuq_ref[...])
    k = _dot(c_kv, wuk_ref[...])
    for hd in range(MLA_HEADS):
        sl = slice(hd * LANES, (hd + 1) * LANES)
        q_ref[:, sl] = _rope_block(q[:, sl], cq, aq, bq, half).astype(q_ref.dtype)
        k_ref[:, sl] = (k[:, sl] + k_pe).astype(k_ref.dtype)
    v_ref[...] = _dot(c_kv, wuv_ref[...]).astype(v_ref.dtype)


def _mla_proj(x2d, g, w_in, g_q, g_kv, w_uq, w_uk, w_uv, q_tables, k_tables, seq_len):
    t = x2d.shape[0]
    tm = ROW_TILE
    pos_blocks = seq_len // tm
    tab_spec = pl.BlockSpec((tm, LANES), lambda i: (i % pos_blocks, 0))

    def whole(arr):
        return pl.BlockSpec(arr.shape, lambda i: (0, 0))

    qk_cols = MLA_HEADS * LANES
    v_cols = MLA_HEADS * MLA_V
    return pl.pallas_call(
        _mla_proj_kernel,
        grid=(t // tm,),
        in_specs=[pl.BlockSpec((tm, D_MODEL), lambda i: (i, 0)),
                  whole(g), whole(w_in), whole(g_q), whole(g_kv),
                  whole(w_uq), whole(w_uk), whole(w_uv)] + [tab_spec] * 6,
        out_specs=[pl.BlockSpec((tm, qk_cols), lambda i: (i, 0)),
                   pl.BlockSpec((tm, qk_cols), lambda i: (i, 0)),
                   pl.BlockSpec((tm, v_cols), lambda i: (i, 0))],
        out_shape=[jax.ShapeDtypeStruct((t, qk_cols), jnp.bfloat16),
                   jax.ShapeDtypeStruct((t, qk_cols), jnp.bfloat16),
                   jax.ShapeDtypeStruct((t, v_cols), jnp.bfloat16)],
        compiler_params=_compiler_params(("parallel",)),
        name="mla_proj",
    )(x2d, g, w_in, g_q, g_kv, w_uq, w_uk, w_uv, *q_tables, *k_tables)


def _mla_attn_kernel(q_ref, k_ref, v_ref, o_ref, vt_ref, wq_ref, m_ref, l_ref, acc_ref):
    tq, tk = ATTN_TQ, ATTN_TK
    _fill_vt(v_ref, vt_ref)

    def q_body(qi, carry):
        q0 = pl.multiple_of(qi * tq, tq)
        for e in range(2):
            wq_ref[e] = _transpose_to_bf16(q_ref[pl.ds(q0, tq), e * LANES:(e + 1) * LANES])
        _init_state(m_ref, l_ref, acc_ref)

        def tile(ki, masked):
            k0 = pl.multiple_of(ki * tk, tk)
            mask = _chunk_mask(k0, q0, tk, tq, tq) if masked else None
            for e in range(2):
                _attn_tile(k_ref[pl.ds(k0, tk), e * LANES:(e + 1) * LANES], wq_ref[e],
                           vt_ref[ki, e * MLA_V:(e + 1) * MLA_V, :],
                           m_ref.at[e], l_ref.at[e], acc_ref.at[e], mask)

        def kv_body(ki, c):
            tile(ki, False)
            return c

        first_diag = qi * (tq // tk)
        lax.fori_loop(0, first_diag, kv_body, 0)
        for d in range(tq // tk):
            tile(first_diag + d, True)

        o = jnp.concatenate([acc_ref[e] * (1.0 / l_ref[e]) for e in range(2)], axis=0)
        o_ref[pl.ds(q0, tq), :] = o.T.astype(o_ref.dtype)
        return carry

    lax.fori_loop(0, q_ref.shape[0] // tq, q_body, 0)


def _mla_attention(q, k, v):
    b, s, _ = q.shape
    tq, tk = ATTN_TQ, ATTN_TK
    pairs = MLA_HEADS // 2
    return pl.pallas_call(
        _mla_attn_kernel,
        grid=(b, pairs),
        in_specs=[
            pl.BlockSpec((None, s, 2 * LANES), lambda bi, hp: (bi, 0, hp)),
            pl.BlockSpec((None, s, 2 * LANES), lambda bi, hp: (bi, 0, hp)),
            pl.BlockSpec((None, s, LANES), lambda bi, hp: (bi, 0, hp)),
        ],
        out_specs=pl.BlockSpec((None, s, LANES), lambda bi, hp: (bi, 0, hp)),
        out_shape=jax.ShapeDtypeStruct((b, s, MLA_HEADS * MLA_V), jnp.bfloat16),
        scratch_shapes=[
            pltpu.VMEM((s // tk, LANES, tk), jnp.bfloat16),
            pltpu.VMEM((2, LANES, tq), jnp.bfloat16),
            pltpu.VMEM((2, 1, tq), jnp.float32),
            pltpu.VMEM((2, 1, tq), jnp.float32),
            pltpu.VMEM((2, MLA_V, tq), jnp.float32),
        ],
        compiler_params=_compiler_params(("parallel", "parallel")),
        name="mla_attn",
    )(q, k, v)


def _out_proj_kernel(x_ref, o_ref, w_ref, y_ref):
    y_ref[...] = x_ref[...] + _dot(o_ref[...], w_ref[...])


def _out_proj(x2d, o2d, w_o):
    t = x2d.shape[0]
    tm = ROW_TILE
    return pl.pallas_call(
        _out_proj_kernel,
        grid=(t // tm,),
        in_specs=[pl.BlockSpec((tm, D_MODEL), lambda i: (i, 0)),
                  pl.BlockSpec((tm, D_MODEL), lambda i: (i, 0)),
                  pl.BlockSpec((D_MODEL, D_MODEL), lambda i: (0, 0))],
        out_specs=pl.BlockSpec((tm, D_MODEL), lambda i: (i, 0)),
        out_shape=jax.ShapeDtypeStruct((t, D_MODEL), jnp.float32),
        compiler_params=_compiler_params(("parallel",)),
        name="out_proj",
    )(x2d, o2d, w_o)


def _ffn_kernel(xh_ref, x_ref, g_ref, wup_ref, cw_ref, wdn_ref, gf_ref, y_ref,
                h_ref, acc_ref, *, pos_blocks, final_norm):
    i = pl.program_id(0)
    g = g_ref[...]
    x = x_ref[...]
    halo = jnp.where(i % pos_blocks != 0, _rms(xh_ref[...], g), 0.0)
    h_ref[:HALO] = halo.astype(h_ref.dtype)
    h_ref[HALO:] = _rms(x, g).astype(h_ref.dtype)
    acc_ref[...] = jnp.zeros(acc_ref.shape, jnp.float32)

    def chunk(f, carry):
        u = _dot(h_ref[...], wup_ref[f])
        cw = cw_ref[f]
        c = (u[HALO:] * cw[2:3]
             + pltpu.roll(u, 1, axis=0)[HALO:] * cw[1:2]
             + pltpu.roll(u, 2, axis=0)[HALO:] * cw[0:1]
             + cw[3:4])
        gate, val = c[:, :FFN_CHUNK], c[:, FFN_CHUNK:]
        act = gate / (1.0 + jnp.exp(-gate)) * val
        acc_ref[...] += _dot(act.astype(jnp.bfloat16), wdn_ref[f])
        return carry

    lax.fori_loop(0, wup_ref.shape[0], chunk, 0)
    y = x + acc_ref[...]
    if final_norm:
        y = _rms(y, gf_ref[...])
    y_ref[...] = y


def _ffn(x2d, g, w_up, cw, w_down, g_final, seq_len, final_norm):
    t = x2d.shape[0]
    tm = ROW_TILE
    pos_blocks = seq_len // tm
    halo_per_tile = tm // HALO

    def whole(arr):
        nd = arr.ndim
        return pl.BlockSpec(arr.shape, lambda i: (0,) * nd)

    return pl.pallas_call(
        functools.partial(_ffn_kernel, pos_blocks=pos_blocks, final_norm=final_norm),
        grid=(t // tm,),
        in_specs=[
            pl.BlockSpec((HALO, D_MODEL),
                         lambda i: (jnp.maximum(i * halo_per_tile - 1, 0), 0)),
            pl.BlockSpec((tm, D_MODEL), lambda i: (i, 0)),
            whole(g), whole(w_up), whole(cw), whole(w_down), whole(g_final),
        ],
        out_specs=pl.BlockSpec((tm, D_MODEL), lambda i: (i, 0)),
        out_shape=jax.ShapeDtypeStruct((t, D_MODEL), jnp.float32),
        scratch_shapes=[
            pltpu.VMEM((HALO + tm, D_MODEL), jnp.bfloat16),
            pltpu.VMEM((tm, D_MODEL), jnp.float32),
        ],
        compiler_params=_compiler_params(("parallel",)),
        name="conv_ffn",
    )(x2d, x2d, g, w_up, cw, w_down, g_final)


def _ffn_weights(w_up, conv_w, conv_b, w_down):
    nf = FFN_HIDDEN // FFN_CHUNK

    def chunks(a):
        lead = a.shape[:-1]
        a = a.reshape(lead + (2, nf, FFN_CHUNK))
        a = jnp.moveaxis(a, -2, 0)
        return a.reshape((nf,) + lead + (2 * FFN_CHUNK,))

    w_up_c = chunks(w_up).astype(jnp.bfloat16)
    taps = chunks(conv_w)
    bias = chunks(conv_b[None])
    cw = jnp.concatenate([taps, bias, jnp.zeros((nf, 4, 2 * FFN_CHUNK), jnp.float32)], axis=1)
    w_dn_c = w_down.reshape(nf, FFN_CHUNK, D_MODEL).astype(jnp.bfloat16)
    return w_up_c, cw, w_dn_c


def kernel(x, attn_norm_g, ffn_norm_g, da_w_qkv, da_lam_q1, da_lam_k1, da_lam_q2, da_lam_k2,
           da_subln_g, da_w_o, mla_w_in, mla_q_norm_g, mla_kv_norm_g, mla_w_uq, mla_w_ukv,
           mla_w_o, ffn_w_up, ffn_conv_w, ffn_conv_b, ffn_w_down, final_norm_g):
    b, s, d = x.shape
    assert d == D_MODEL and s % ROW_TILE == 0 and s % ATTN_TQ == 0 and ATTN_TQ % ATTN_TK == 0
    t = b * s
    bf16 = jnp.bfloat16
    x2d = x.reshape(t, d)
    g_final = final_norm_g.reshape(1, d)

    da_tables = _rope_tables(s, DA_ROT, 0, DA_HEAD_DIM)
    mla_k_tables = _rope_tables(s, MLA_ROPE, MLA_NOPE, LANES)
    mla_scale = (MLA_NOPE + MLA_ROPE) ** -0.5
    mla_q_tables = tuple(tb * mla_scale for tb in mla_k_tables)

    for i in range(DEPTH):
        j = i // N_MIXERS
        g_attn = attn_norm_g[i].reshape(1, d)
        if i % N_MIXERS == 0:
            lambda_init = 0.8 - 0.6 * float(np.exp(-0.3 * i))
            qkv = _da_proj(x2d, g_attn, da_w_qkv[j].astype(bf16), da_tables, s)
            o = _da_attention(qkv.reshape(b, s, 3 * d),
                              da_lam_q1[j].reshape(1, -1), da_lam_k1[j].reshape(1, -1),
                              da_lam_q2[j].reshape(1, -1), da_lam_k2[j].reshape(1, -1),
                              da_subln_g[j].reshape(1, -1), lambda_init)
            w_o = da_w_o[j]
        else:
            pe_pad = jnp.zeros((d, LANES), jnp.float32)
            pe_lo = MLA_Q_RANK + MLA_KV_RANK
            pe_pad = pe_pad.at[:, MLA_NOPE:MLA_NOPE + MLA_ROPE].set(mla_w_in[j][:, pe_lo:])
            w_in = jnp.concatenate([mla_w_in[j][:, :pe_lo], pe_pad], axis=1).astype(bf16)
            qk_dim = MLA_NOPE + MLA_ROPE
            w_uq = mla_w_uq[j].reshape(MLA_Q_RANK, MLA_HEADS, qk_dim)
            w_uq = jnp.pad(w_uq, ((0, 0), (0, 0), (0, LANES - qk_dim)))
            w_uq = w_uq.reshape(MLA_Q_RANK, MLA_HEADS * LANES).astype(bf16)
            w_ukv = mla_w_ukv[j].reshape(MLA_KV_RANK, MLA_HEADS, MLA_NOPE + MLA_V)
            w_uk = jnp.pad(w_ukv[:, :, :MLA_NOPE], ((0, 0), (0, 0), (0, LANES - MLA_NOPE)))
            w_uk = w_uk.reshape(MLA_KV_RANK, MLA_HEADS * LANES).astype(bf16)
            w_uv = w_ukv[:, :, MLA_NOPE:].reshape(MLA_KV_RANK, MLA_HEADS * MLA_V).astype(bf16)
            q, k, v = _mla_proj(x2d, g_attn, w_in, mla_q_norm_g[j].reshape(1, -1),
                                mla_kv_norm_g[j].reshape(1, -1), w_uq, w_uk, w_uv,
                                mla_q_tables, mla_k_tables, s)
            o = _mla_attention(q.reshape(b, s, -1), k.reshape(b, s, -1), v.reshape(b, s, -1))
            w_o = mla_w_o[j]
        x2d = _out_proj(x2d, o.reshape(t, d), w_o.astype(bf16))
        w_up_c, cw, w_dn_c = _ffn_weights(ffn_w_up[i], ffn_conv_w[i], ffn_conv_b[i],
                                          ffn_w_down[i])
        x2d = _ffn(x2d, ffn_norm_g[i].reshape(1, d), w_up_c, cw, w_dn_c, g_final, s,
                   final_norm=(i == DEPTH - 1))
    return x2d.reshape(b, s, d)
```

```python
import functools

import numpy as np
import jax
import jax.numpy as jnp
from jax import lax
from jax.experimental import pallas as pl
from jax.experimental.pallas import tpu as pltpu

D_MODEL = 1024
DEPTH = 2
CHUNK = 64
ROPE_THETA = 500000.0
NORM_EPS = 1e-6
N_MIXERS = 2

DA_HEADS = 8
DA_HEAD_DIM = D_MODEL // (2 * DA_HEADS)
DA_ROT = DA_HEAD_DIM // 4

MLA_HEADS = 16
MLA_Q_RANK = 384
MLA_KV_RANK = 256
MLA_NOPE = 64
MLA_ROPE = 32
MLA_V = 64

FFN_HIDDEN = 2816
CONV_WIDTH = 3

LANES = 128
BF16_SUBLANES = 16
MASK_VALUE = -1e30

ROW_TILE = 512
ATTN_TQ = 1024
ATTN_TILES_PER_ITER = 2
ATTN_TK = 256
ATTN_G = 256
SUM_ROWS = BF16_SUBLANES
ATTN_LOOKAHEAD = 8
CHUNK_SHIFT = CHUNK.bit_length() - 1
LOG2_E = 1.4426950408889634
FFN_CHUNK = 256
HALO = BF16_SUBLANES

VMEM_LIMIT = 56 * 1024 * 1024


def _compiler_params(semantics):
    return pltpu.CompilerParams(dimension_semantics=semantics,
                                vmem_limit_bytes=VMEM_LIMIT)


def _rms(x, g):
    return x * lax.rsqrt(jnp.mean(x * x, axis=-1, keepdims=True) + NORM_EPS) * g


def _dot(a, b):
    return jnp.dot(a, b, preferred_element_type=jnp.float32)


def _rope_block(x, c, a, bm, half):
    return (x * c + pltpu.roll(x, half, axis=1) * a
            + pltpu.roll(x, LANES - half, axis=1) * bm)


def _rope_tables(seq_len, rot_dim, lane_start, period):
    half = rot_dim // 2
    inv = ROPE_THETA ** (-jnp.arange(0, rot_dim, 2, dtype=jnp.float32) / rot_dim)
    ang = jnp.arange(seq_len, dtype=jnp.float32)[:, None] * inv[None, :]
    cos, sin = jnp.cos(ang), jnp.sin(ang)
    r = np.arange(LANES) % period - lane_start
    first = (r >= 0) & (r < half)
    second = (r >= half) & (r < rot_dim)
    idx = np.where(first, r, np.where(second, r - half, 0))
    cos_l, sin_l = cos[:, idx], sin[:, idx]
    c = jnp.where(first | second, cos_l, 1.0)
    a = jnp.where(second, sin_l, 0.0)
    bm = jnp.where(first, -sin_l, 0.0)
    return c, a, bm


def _da_proj_kernel(x_ref, g_ref, w_ref, c_ref, a_ref, bm_ref, o_ref):
    h = _rms(x_ref[...], g_ref[...]).astype(jnp.bfloat16)
    c, a, bm = c_ref[...], a_ref[...], bm_ref[...]
    scale = DA_HEAD_DIM ** -0.5 * LOG2_E
    for part in range(3):
        lo = part * D_MODEL
        y = _dot(h, w_ref[:, lo:lo + D_MODEL])
        for blk in range(D_MODEL // LANES):
            yb = y[:, blk * LANES:(blk + 1) * LANES]
            if part < 2:
                yb = _rope_block(yb, c, a, bm, DA_ROT // 2)
            if part == 0:
                yb = yb * scale
            o_ref[:, lo + blk * LANES:lo + (blk + 1) * LANES] = yb.astype(o_ref.dtype)


def _da_proj(x2d, g, w_qkv, tables, seq_len):
    t = x2d.shape[0]
    tm = ROW_TILE
    pos_blocks = seq_len // tm
    tab_spec = pl.BlockSpec((tm, LANES), lambda i: (i % pos_blocks, 0))
    return pl.pallas_call(
        _da_proj_kernel,
        grid=(t // tm,),
        in_specs=[
            pl.BlockSpec((tm, D_MODEL), lambda i: (i, 0)),
            pl.BlockSpec((1, D_MODEL), lambda i: (0, 0)),
            pl.BlockSpec((D_MODEL, 3 * D_MODEL), lambda i: (0, 0)),
            tab_spec, tab_spec, tab_spec,
        ],
        out_specs=pl.BlockSpec((tm, 3 * D_MODEL), lambda i: (i, 0)),
        out_shape=jax.ShapeDtypeStruct((t, 3 * D_MODEL), jnp.bfloat16),
        compiler_params=_compiler_params(("parallel",)),
        name="da_proj",
    )(x2d, g, w_qkv, *tables)


def _softmax_pv(s, vt1, m_prev, acc_ref):
    m_new = jnp.maximum(m_prev, jnp.max(s, axis=0, keepdims=True))
    alpha = jnp.exp2(m_prev - m_new)
    p = jnp.exp2(s - m_new).astype(vt1.dtype)
    acc_ref[...] = alpha * acc_ref[...] + _dot(vt1, p)
    return m_new


def _attn_q_tile(qi, groups, mask_ref, acc_ref, k_tile, wq_group, vt_tile):
    halves = ATTN_TQ // ATTN_G
    assert ATTN_TK == ATTN_G and halves % ATTN_TILES_PER_ITER == 0

    def scores(ki, slot, masked):
        stream, half = groups[slot]
        s = _dot(k_tile(ki, stream), wq_group(stream, half))
        if masked:
            s = jnp.where(mask_ref[...] > 0.0, s, MASK_VALUE)
        return s

    def run(items, ms):
        ms = list(ms)
        pending = [scores(*item) for item in items[:ATTN_LOOKAHEAD]]
        for n, (ki, slot, _) in enumerate(items):
            s = pending.pop(0)
            if n + ATTN_LOOKAHEAD < len(items):
                pending.append(scores(*items[n + ATTN_LOOKAHEAD]))
            ms[slot] = _softmax_pv(s, vt_tile(ki, groups[slot][0]), ms[slot], acc_ref.at[slot])
        return tuple(ms)

    def full_tiles(j, ms):
        return run([(ATTN_TILES_PER_ITER * j + t, slot, False)
                    for t in range(ATTN_TILES_PER_ITER) for slot in range(len(groups))], ms)

    acc_ref[...] = jnp.zeros(acc_ref.shape, jnp.float32)
    ms = tuple(jnp.full((1, ATTN_G), MASK_VALUE, jnp.float32) for _ in groups)
    ms = lax.fori_loop(0, qi * (halves // ATTN_TILES_PER_ITER), full_tiles, ms)
    diag = [(halves * qi + d, slot, half == d)
            for d in range(halves) for slot, (_, half) in enumerate(groups) if half >= d]
    run(diag, ms)


def _fill_tri_mask(mask_ref):
    kc = lax.broadcasted_iota(jnp.int32, mask_ref.shape, 0) >> CHUNK_SHIFT
    qc = lax.broadcasted_iota(jnp.int32, mask_ref.shape, 1) >> CHUNK_SHIFT
    mask_ref[...] = (kc <= qc).astype(mask_ref.dtype)


def _transpose_to_bf16(x):
    return x.astype(jnp.float32).T.astype(jnp.bfloat16)


def _da_attn_kernel(lq1_ref, lk1_ref, lq2_ref, lk2_ref, sg_ref, q_ref, k_ref, v_ref,
                    o_ref, vt_ref, wq_ref, mask_ref, acc_ref, *, lambda_init):
    tq, tk, e = ATTN_TQ, ATTN_TK, 2 * DA_HEAD_DIM
    halves = tq // ATTN_G
    _fill_tri_mask(mask_ref)

    def vt_body(ki, carry):
        k0 = pl.multiple_of(ki * tk, tk)
        vt_ref[ki, :e] = _transpose_to_bf16(v_ref[pl.ds(k0, tk), :])
        vt_ref[ki, e:] = jnp.ones((SUM_ROWS, tk), vt_ref.dtype)
        return carry

    lax.fori_loop(0, vt_ref.shape[0], vt_body, 0)
    lam = (jnp.exp(jnp.sum(lq1_ref[...] * lk1_ref[...], keepdims=True))
           - jnp.exp(jnp.sum(lq2_ref[...] * lk2_ref[...], keepdims=True))
           + lambda_init)
    groups = tuple((comp, half) for comp in range(2) for half in range(halves))

    def q_body(qi, carry):
        q0 = pl.multiple_of(qi * tq, tq)
        qt = q_ref[pl.ds(q0, tq), :].astype(jnp.float32).T
        row = lax.broadcasted_iota(jnp.int32, qt.shape, 0)
        wq_ref[0] = jnp.where(row < DA_HEAD_DIM, qt, 0.0).astype(wq_ref.dtype)
        wq_ref[1] = jnp.where(row >= DA_HEAD_DIM, qt, 0.0).astype(wq_ref.dtype)
        _attn_q_tile(
            qi, groups, mask_ref, acc_ref,
            k_tile=lambda ki, comp: k_ref[pl.ds(pl.multiple_of(ki * tk, tk), tk), :],
            wq_group=lambda comp, half: wq_ref[comp, :, half * ATTN_G:(half + 1) * ATTN_G],
            vt_tile=lambda ki, comp: vt_ref[ki])
        for half in range(halves):
            a0, a1 = acc_ref[half], acc_ref[halves + half]
            o = a0[:e] * (1.0 / a0[e:e + 1]) - lam * (a1[:e] * (1.0 / a1[e:e + 1]))
            o = _rms(o.T, sg_ref[...]) * (1.0 - lambda_init)
            r0 = pl.multiple_of(q0 + half * ATTN_G, ATTN_G)
            o_ref[pl.ds(r0, ATTN_G), :] = o.astype(o_ref.dtype)
        return carry

    lax.fori_loop(0, q_ref.shape[0] // tq, q_body, 0)


def _da_attention(qkv, lq1, lk1, lq2, lk2, subln_g, lambda_init):
    b, s, _ = qkv.shape
    tq, tk = ATTN_TQ, ATTN_TK
    hd = 2 * DA_HEAD_DIM
    vec = pl.BlockSpec((1, DA_HEAD_DIM), lambda bi, h: (0, 0))
    return pl.pallas_call(
        functools.partial(_da_attn_kernel, lambda_init=lambda_init),
        grid=(b, DA_HEADS),
        in_specs=[
            vec, vec, vec, vec,
            pl.BlockSpec((1, hd), lambda bi, h: (0, 0)),
            pl.BlockSpec((None, s, hd), lambda bi, h: (bi, 0, h)),
            pl.BlockSpec((None, s, hd), lambda bi, h: (bi, 0, DA_HEADS + h)),
            pl.BlockSpec((None, s, hd), lambda bi, h: (bi, 0, 2 * DA_HEADS + h)),
        ],
        out_specs=pl.BlockSpec((None, s, hd), lambda bi, h: (bi, 0, h)),
        out_shape=jax.ShapeDtypeStruct((b, s, DA_HEADS * hd), jnp.bfloat16),
        scratch_shapes=[
            pltpu.VMEM((s // tk, hd + SUM_ROWS, tk), jnp.bfloat16),
            pltpu.VMEM((2, hd, tq), jnp.bfloat16),
            pltpu.VMEM((tk, ATTN_G), jnp.float32),
            pltpu.VMEM((2 * tq // ATTN_G, hd + SUM_ROWS, ATTN_G), jnp.float32),
        ],
        compiler_params=_compiler_params(("parallel", "parallel")),
        name="da_attn",
    )(lq1, lk1, lq2, lk2, subln_g, qkv, qkv, qkv)


def _mla_proj_kernel(x_ref, g_ref, win_ref, gq_ref, gkv_ref, wuq_ref, wuk_ref, wuv_ref,
                     cq_ref, aq_ref, bq_ref, ck_ref, ak_ref, bk_ref,
                     q_ref, k_ref, v_ref):
    h = _rms(x_ref[...], g_ref[...]).astype(jnp.bfloat16)
    proj = _dot(h, win_ref[...])
    kv_lo = MLA_Q_RANK
    pe_lo = MLA_Q_RANK + MLA_KV_RANK
    c_q = _rms(proj[:, :kv_lo], gq_ref[...]).astype(jnp.bfloat16)
    c_kv = _rms(proj[:, kv_lo:pe_lo], gkv_ref[...]).astype(jnp.bfloat16)
    half = MLA_ROPE // 2
    k_pe = _rope_block(proj[:, pe_lo:pe_lo + LANES], ck_ref[...], ak_ref[...], bk_ref[...], half)
    cq, aq, bq = cq_ref[...], aq_ref[...], bq_ref[...]
    q = _dot(c_q, wuq_ref[...])
    k = _dot(c_kv, wuk_ref[...])
    for hd in range(MLA_HEADS):
        sl = slice(hd * LANES, (hd + 1) * LANES)
        q_ref[:, sl] = _rope_block(q[:, sl], cq, aq, bq, half).astype(q_ref.dtype)
        k_ref[:, sl] = (k[:, sl] + k_pe).astype(k_ref.dtype)
    v_ref[...] = _dot(c_kv, wuv_ref[...]).astype(v_ref.dtype)


def _mla_proj(x2d, g, w_in, g_q, g_kv, w_uq, w_uk, w_uv, q_tables, k_tables, seq_len):
    t = x2d.shape[0]
    tm = ROW_TILE
    pos_blocks = seq_len // tm
    tab_spec = pl.BlockSpec((tm, LANES), lambda i: (i % pos_blocks, 0))

    def whole(arr):
        return pl.BlockSpec(arr.shape, lambda i: (0, 0))

    qk_cols = MLA_HEADS * LANES
    v_cols = MLA_HEADS * MLA_V
    return pl.pallas_call(
        _mla_proj_kernel,
        grid=(t // tm,),
        in_specs=[pl.BlockSpec((tm, D_MODEL), lambda i: (i, 0)),
                  whole(g), whole(w_in), whole(g_q), whole(g_kv),
                  whole(w_uq), whole(w_uk), whole(w_uv)] + [tab_spec] * 6,
        out_specs=[pl.BlockSpec((tm, qk_cols), lambda i: (i, 0)),
                   pl.BlockSpec((tm, qk_cols), lambda i: (i, 0)),
                   pl.BlockSpec((tm, v_cols), lambda i: (i, 0))],
        out_shape=[jax.ShapeDtypeStruct((t, qk_cols), jnp.bfloat16),
                   jax.ShapeDtypeStruct((t, qk_cols), jnp.bfloat16),
                   jax.ShapeDtypeStruct((t, v_cols), jnp.bfloat16)],
        compiler_params=_compiler_params(("parallel",)),
        name="mla_proj",
    )(x2d, g, w_in, g_q, g_kv, w_uq, w_uk, w_uv, *q_tables, *k_tables)


def _mla_attn_kernel(q_ref, k_ref, v_ref, o_ref, vt_ref, wq_ref, mask_ref, acc_ref):
    tq, tk, e = ATTN_TQ, ATTN_TK, MLA_V
    halves = tq // ATTN_G
    _fill_tri_mask(mask_ref)

    def vt_body(ki, carry):
        k0 = pl.multiple_of(ki * tk, tk)
        vt = _transpose_to_bf16(v_ref[pl.ds(k0, tk), :])
        for head in range(2):
            vt_ref[ki, head, :e] = vt[head * e:(head + 1) * e]
            vt_ref[ki, head, e:] = jnp.ones((SUM_ROWS, tk), vt_ref.dtype)
        return carry

    lax.fori_loop(0, vt_ref.shape[0], vt_body, 0)
    groups = tuple((head, half) for head in range(2) for half in range(halves))

    def q_body(qi, carry):
        q0 = pl.multiple_of(qi * tq, tq)
        for head in range(2):
            wq_ref[head] = _transpose_to_bf16(
                q_ref[pl.ds(q0, tq), head * LANES:(head + 1) * LANES])
        _attn_q_tile(
            qi, groups, mask_ref, acc_ref,
            k_tile=lambda ki, head: k_ref[pl.ds(pl.multiple_of(ki * tk, tk), tk),
                                          head * LANES:(head + 1) * LANES],
            wq_group=lambda head, half: wq_ref[head, :, half * ATTN_G:(half + 1) * ATTN_G],
            vt_tile=lambda ki, head: vt_ref[ki, head])
        for half in range(halves):
            parts = []
            for head in range(2):
                a = acc_ref[head * halves + half]
                parts.append(a[:e] * (1.0 / a[e:e + 1]))
            r0 = pl.multiple_of(q0 + half * ATTN_G, ATTN_G)
            o_ref[pl.ds(r0, ATTN_G), :] = jnp.concatenate(parts, axis=0).T.astype(o_ref.dtype)
        return carry

    lax.fori_loop(0, q_ref.shape[0] // tq, q_body, 0)


def _mla_attention(q, k, v):
    b, s, _ = q.shape
    tq, tk = ATTN_TQ, ATTN_TK
    pairs = MLA_HEADS // 2
    return pl.pallas_call(
        _mla_attn_kernel,
        grid=(b, pairs),
        in_specs=[
            pl.BlockSpec((None, s, 2 * LANES), lambda bi, hp: (bi, 0, hp)),
            pl.BlockSpec((None, s, 2 * LANES), lambda bi, hp: (bi, 0, hp)),
            pl.BlockSpec((None, s, LANES), lambda bi, hp: (bi, 0, hp)),
        ],
        out_specs=pl.BlockSpec((None, s, LANES), lambda bi, hp: (bi, 0, hp)),
        out_shape=jax.ShapeDtypeStruct((b, s, MLA_HEADS * MLA_V), jnp.bfloat16),
        scratch_shapes=[
            pltpu.VMEM((s // tk, 2, MLA_V + SUM_ROWS, tk), jnp.bfloat16),
            pltpu.VMEM((2, LANES, tq), jnp.bfloat16),
            pltpu.VMEM((tk, ATTN_G), jnp.float32),
            pltpu.VMEM((2 * tq // ATTN_G, MLA_V + SUM_ROWS, ATTN_G), jnp.float32),
        ],
        compiler_params=_compiler_params(("parallel", "parallel")),
        name="mla_attn",
    )(q, k, v)


def _out_proj_kernel(x_ref, o_ref, w_ref, y_ref):
    y_ref[...] = x_ref[...] + _dot(o_ref[...], w_ref[...])


def _out_proj(x2d, o2d, w_o):
    t = x2d.shape[0]
    tm = ROW_TILE
    return pl.pallas_call(
        _out_proj_kernel,
        grid=(t // tm,),
        in_specs=[pl.BlockSpec((tm, D_MODEL), lambda i: (i, 0)),
                  pl.BlockSpec((tm, D_MODEL), lambda i: (i, 0)),
                  pl.BlockSpec((D_MODEL, D_MODEL), lambda i: (0, 0))],
        out_specs=pl.BlockSpec((tm, D_MODEL), lambda i: (i, 0)),
        out_shape=jax.ShapeDtypeStruct((t, D_MODEL), jnp.float32),
        compiler_params=_compiler_params(("parallel",)),
        name="out_proj",
    )(x2d, o2d, w_o)


def _ffn_kernel(xh_ref, x_ref, g_ref, wup_ref, cw_ref, wdn_ref, gf_ref, y_ref,
                h_ref, u_ref, acc_ref, *, pos_blocks, final_norm):
    i = pl.program_id(0)
    g = g_ref[...]
    x = x_ref[...]
    halo = jnp.where(i % pos_blocks != 0, _rms(xh_ref[...], g), 0.0)
    h_ref[:HALO] = halo.astype(h_ref.dtype)
    h_ref[HALO:] = _rms(x, g).astype(h_ref.dtype)
    acc_ref[...] = jnp.zeros(acc_ref.shape, jnp.float32)
    tm = x.shape[0]
    nf = wup_ref.shape[0]

    def up(f, slot):
        u_ref[slot] = _dot(h_ref[...], wup_ref[f])

    def down(f, slot):
        u = u_ref.at[slot]
        cw = cw_ref[f]
        c = (u[pl.ds(HALO, tm), :] * cw[2:3] + u[pl.ds(HALO - 1, tm), :] * cw[1:2]
             + u[pl.ds(HALO - 2, tm), :] * cw[0:1] + cw[3:4])
        gate, val = c[:, :FFN_CHUNK], c[:, FFN_CHUNK:]
        act = gate / (1.0 + jnp.exp(-gate)) * val
        acc_ref[...] += _dot(act.astype(jnp.bfloat16), wdn_ref[f])

    def chunk_pair(j, carry):
        f = 2 * j
        up(f + 1, 1)
        down(f, 0)
        up(f + 2, 0)
        down(f + 1, 1)
        return carry

    assert nf % 2 == 1
    up(0, 0)
    lax.fori_loop(0, nf // 2, chunk_pair, 0)
    down(nf - 1, 0)
    y = x + acc_ref[...]
    if final_norm:
        y = _rms(y, gf_ref[...])
    y_ref[...] = y


def _ffn(x2d, g, w_up, cw, w_down, g_final, seq_len, final_norm):
    t = x2d.shape[0]
    tm = ROW_TILE
    pos_blocks = seq_len // tm
    halo_per_tile = tm // HALO

    def whole(arr):
        nd = arr.ndim
        return pl.BlockSpec(arr.shape, lambda i: (0,) * nd)

    return pl.pallas_call(
        functools.partial(_ffn_kernel, pos_blocks=pos_blocks, final_norm=final_norm),
        grid=(t // tm,),
        in_specs=[
            pl.BlockSpec((HALO, D_MODEL),
                         lambda i: (jnp.maximum(i * halo_per_tile - 1, 0), 0)),
            pl.BlockSpec((tm, D_MODEL), lambda i: (i, 0)),
            whole(g), whole(w_up), whole(cw), whole(w_down), whole(g_final),
        ],
        out_specs=pl.BlockSpec((tm, D_MODEL), lambda i: (i, 0)),
        out_shape=jax.ShapeDtypeStruct((t, D_MODEL), jnp.float32),
        scratch_shapes=[
            pltpu.VMEM((HALO + tm, D_MODEL), jnp.bfloat16),
            pltpu.VMEM((2, HALO + tm, 2 * FFN_CHUNK), jnp.float32),
            pltpu.VMEM((tm, D_MODEL), jnp.float32),
        ],
        compiler_params=_compiler_params(("parallel",)),
        name="conv_ffn",
    )(x2d, x2d, g, w_up, cw, w_down, g_final)


def _ffn_weights(w_up, conv_w, conv_b, w_down):
    nf = FFN_HIDDEN // FFN_CHUNK

    def chunks(a):
        lead = a.shape[:-1]
        a = a.reshape(lead + (2, nf, FFN_CHUNK))
        a = jnp.moveaxis(a, -2, 0)
        return a.reshape((nf,) + lead + (2 * FFN_CHUNK,))

    w_up_c = chunks(w_up).astype(jnp.bfloat16)
    taps = chunks(conv_w)
    bias = chunks(conv_b[None])
    cw = jnp.concatenate([taps, bias, jnp.zeros((nf, 4, 2 * FFN_CHUNK), jnp.float32)], axis=1)
    w_dn_c = w_down.reshape(nf, FFN_CHUNK, D_MODEL).astype(jnp.bfloat16)
    return w_up_c, cw, w_dn_c


def kernel(x, attn_norm_g, ffn_norm_g, da_w_qkv, da_lam_q1, da_lam_k1, da_lam_q2, da_lam_k2,
           da_subln_g, da_w_o, mla_w_in, mla_q_norm_g, mla_kv_norm_g, mla_w_uq, mla_w_ukv,
           mla_w_o, ffn_w_up, ffn_conv_w, ffn_conv_b, ffn_w_down, final_norm_g):
    b, s, d = x.shape
    assert d == D_MODEL and s % ROW_TILE == 0 and s % ATTN_TQ == 0 and ATTN_TQ % ATTN_TK == 0
    t = b * s
    bf16 = jnp.bfloat16
    x2d = x.reshape(t, d)
    g_final = final_norm_g.reshape(1, d)

    da_tables = _rope_tables(s, DA_ROT, 0, DA_HEAD_DIM)
    mla_k_tables = _rope_tables(s, MLA_ROPE, MLA_NOPE, LANES)
    mla_scale = (MLA_NOPE + MLA_ROPE) ** -0.5 * LOG2_E
    mla_q_tables = tuple(tb * mla_scale for tb in mla_k_tables)

    for i in range(DEPTH):
        j = i // N_MIXERS
        g_attn = attn_norm_g[i].reshape(1, d)
        if i % N_MIXERS == 0:
            lambda_init = 0.8 - 0.6 * float(np.exp(-0.3 * i))
            qkv = _da_proj(x2d, g_attn, da_w_qkv[j].astype(bf16), da_tables, s)
            o = _da_attention(qkv.reshape(b, s, 3 * d),
                              da_lam_q1[j].reshape(1, -1), da_lam_k1[j].reshape(1, -1),
                              da_lam_q2[j].reshape(1, -1), da_lam_k2[j].reshape(1, -1),
                              da_subln_g[j].reshape(1, -1), lambda_init)
            w_o = da_w_o[j]
        else:
            pe_pad = jnp.zeros((d, LANES), jnp.float32)
            pe_lo = MLA_Q_RANK + MLA_KV_RANK
            pe_pad = pe_pad.at[:, MLA_NOPE:MLA_NOPE + MLA_ROPE].set(mla_w_in[j][:, pe_lo:])
            w_in = jnp.concatenate([mla_w_in[j][:, :pe_lo], pe_pad], axis=1).astype(bf16)
            qk_dim = MLA_NOPE + MLA_ROPE
            w_uq = mla_w_uq[j].reshape(MLA_Q_RANK, MLA_HEADS, qk_dim)
            w_uq = jnp.pad(w_uq, ((0, 0), (0, 0), (0, LANES - qk_dim)))
            w_uq = w_uq.reshape(MLA_Q_RANK, MLA_HEADS * LANES).astype(bf16)
            w_ukv = mla_w_ukv[j].reshape(MLA_KV_RANK, MLA_HEADS, MLA_NOPE + MLA_V)
            w_uk = jnp.pad(w_ukv[:, :, :MLA_NOPE], ((0, 0), (0, 0), (0, LANES - MLA_NOPE)))
            w_uk = w_uk.reshape(MLA_KV_RANK, MLA_HEADS * LANES).astype(bf16)
            w_uv = w_ukv[:, :, MLA_NOPE:].reshape(MLA_KV_RANK, MLA_HEADS * MLA_V).astype(bf16)
            q, k, v = _mla_proj(x2d, g_attn, w_in, mla_q_norm_g[j].reshape(1, -1),
                                mla_kv_norm_g[j].reshape(1, -1), w_uq, w_uk, w_uv,
                                mla_q_tables, mla_k_tables, s)
            o = _mla_attention(q.reshape(b, s, -1), k.reshape(b, s, -1), v.reshape(b, s, -1))
            w_o = mla_w_o[j]
        x2d = _out_proj(x2d, o.reshape(t, d), w_o.astype(bf16))
        w_up_c, cw, w_dn_c = _ffn_weights(ffn_w_up[i], ffn_conv_w[i], ffn_conv_b[i],
                                          ffn_w_down[i])
        x2d = _ffn(x2d, ffn_norm_g[i].reshape(1, d), w_up_c, cw, w_dn_c, g_final, s,
                   final_norm=(i == DEPTH - 1))
    return x2d.reshape(b, s, d)
```

```python
import functools

import numpy as np
import jax
import jax.numpy as jnp
from jax import lax
from jax.experimental import pallas as pl
from jax.experimental.pallas import tpu as pltpu

D_MODEL = 1024
DEPTH = 2
CHUNK = 64
ROPE_THETA = 500000.0
NORM_EPS = 1e-6
N_MIXERS = 2

DA_HEADS = 8
DA_HEAD_DIM = D_MODEL // (2 * DA_HEADS)
DA_ROT = DA_HEAD_DIM // 4

MLA_HEADS = 16
MLA_Q_RANK = 384
MLA_KV_RANK = 256
MLA_NOPE = 64
MLA_ROPE = 32
MLA_V = 64

FFN_HIDDEN = 2816
CONV_WIDTH = 3

LANES = 128
BF16_SUBLANES = 16
MASK_VALUE = -1e30

ROW_TILE = 512
ATTN_TQ = 1024
ATTN_TILES_PER_ITER = 4
ATTN_TK = 256
ATTN_G = 256
SUM_ROWS = BF16_SUBLANES
ATTN_LOOKAHEAD = 8
CHUNK_SHIFT = CHUNK.bit_length() - 1
LOG2_E = 1.4426950408889634
FFN_CHUNK = 256
FFN_RING = 3
HALO = BF16_SUBLANES

VMEM_LIMIT = 56 * 1024 * 1024


def _compiler_params(semantics):
    return pltpu.CompilerParams(dimension_semantics=semantics,
                                vmem_limit_bytes=VMEM_LIMIT)


def _rms(x, g):
    return x * lax.rsqrt(jnp.mean(x * x, axis=-1, keepdims=True) + NORM_EPS) * g


def _dot(a, b):
    return jnp.dot(a, b, preferred_element_type=jnp.float32)


def _rope_block(x, c, a, bm, half):
    return (x * c + pltpu.roll(x, half, axis=1) * a
            + pltpu.roll(x, LANES - half, axis=1) * bm)


def _rope_tables(seq_len, rot_dim, lane_start, period):
    half = rot_dim // 2
    inv = ROPE_THETA ** (-jnp.arange(0, rot_dim, 2, dtype=jnp.float32) / rot_dim)
    ang = jnp.arange(seq_len, dtype=jnp.float32)[:, None] * inv[None, :]
    cos, sin = jnp.cos(ang), jnp.sin(ang)
    r = np.arange(LANES) % period - lane_start
    first = (r >= 0) & (r < half)
    second = (r >= half) & (r < rot_dim)
    idx = np.where(first, r, np.where(second, r - half, 0))
    cos_l, sin_l = cos[:, idx], sin[:, idx]
    c = jnp.where(first | second, cos_l, 1.0)
    a = jnp.where(second, sin_l, 0.0)
    bm = jnp.where(first, -sin_l, 0.0)
    return c, a, bm


def _da_proj_kernel(x_ref, g_ref, w_ref, c_ref, a_ref, bm_ref, o_ref):
    h = _rms(x_ref[...], g_ref[...]).astype(jnp.bfloat16)
    c, a, bm = c_ref[...], a_ref[...], bm_ref[...]
    scale = DA_HEAD_DIM ** -0.5 * LOG2_E
    for part in range(3):
        lo = part * D_MODEL
        y = _dot(h, w_ref[:, lo:lo + D_MODEL])
        for blk in range(D_MODEL // LANES):
            yb = y[:, blk * LANES:(blk + 1) * LANES]
            if part < 2:
                yb = _rope_block(yb, c, a, bm, DA_ROT // 2)
            if part == 0:
                yb = yb * scale
            o_ref[:, lo + blk * LANES:lo + (blk + 1) * LANES] = yb.astype(o_ref.dtype)


def _da_proj(x2d, g, w_qkv, tables, seq_len):
    t = x2d.shape[0]
    tm = ROW_TILE
    pos_blocks = seq_len // tm
    tab_spec = pl.BlockSpec((tm, LANES), lambda i: (i % pos_blocks, 0))
    return pl.pallas_call(
        _da_proj_kernel,
        grid=(t // tm,),
        in_specs=[
            pl.BlockSpec((tm, D_MODEL), lambda i: (i, 0)),
            pl.BlockSpec((1, D_MODEL), lambda i: (0, 0)),
            pl.BlockSpec((D_MODEL, 3 * D_MODEL), lambda i: (0, 0)),
            tab_spec, tab_spec, tab_spec,
        ],
        out_specs=pl.BlockSpec((tm, 3 * D_MODEL), lambda i: (i, 0)),
        out_shape=jax.ShapeDtypeStruct((t, 3 * D_MODEL), jnp.bfloat16),
        compiler_params=_compiler_params(("parallel",)),
        name="da_proj",
    )(x2d, g, w_qkv, *tables)


def _softmax_pv(s, vt1, m_prev, acc_ref):
    m_new = jnp.maximum(m_prev, jnp.max(s, axis=0, keepdims=True))
    alpha = jnp.exp2(m_prev - m_new)
    p = jnp.exp2(s - m_new).astype(vt1.dtype)
    acc_ref[...] = alpha * acc_ref[...] + _dot(vt1, p)
    return m_new


def _attn_q_tile(qi, groups, mask_ref, acc_ref, k_tile, wq_group, vt_tile, fill_vt,
                 finish_half):
    halves = ATTN_TQ // ATTN_G
    assert ATTN_TK == ATTN_G and halves % ATTN_TILES_PER_ITER == 0

    def scores(ki, slot, masked):
        stream, half = groups[slot]
        s = _dot(k_tile(ki, stream), wq_group(stream, half))
        if masked:
            s = jnp.where(mask_ref[...] > 0.0, s, MASK_VALUE)
        return s

    def run(items, ms, after=None):
        ms = list(ms)
        pending = [scores(*item) for item in items[:ATTN_LOOKAHEAD]]
        for n, (ki, slot, _) in enumerate(items):
            s = pending.pop(0)
            if n + ATTN_LOOKAHEAD < len(items):
                pending.append(scores(*items[n + ATTN_LOOKAHEAD]))
            ms[slot] = _softmax_pv(s, vt_tile(ki, groups[slot][0]), ms[slot], acc_ref.at[slot])
            if after and n in after:
                after[n]()
        return tuple(ms)

    def full_tiles(j, ms):
        return run([(ATTN_TILES_PER_ITER * j + t, slot, False)
                    for t in range(ATTN_TILES_PER_ITER) for slot in range(len(groups))], ms)

    acc_ref[...] = jnp.zeros(acc_ref.shape, jnp.float32)
    ms = tuple(jnp.full((1, ATTN_G), MASK_VALUE, jnp.float32) for _ in groups)
    ms = lax.fori_loop(0, qi * (halves // ATTN_TILES_PER_ITER), full_tiles, ms)
    diag, after = [], {}
    for d in range(halves):
        fill_vt(halves * qi + d)
        diag += [(halves * qi + d, slot, half == d)
                 for slot, (_, half) in enumerate(groups) if half >= d]
        after[len(diag) - 1] = functools.partial(finish_half, d)
    run(diag, ms, after)


def _fill_tri_mask(mask_ref):
    kc = lax.broadcasted_iota(jnp.int32, mask_ref.shape, 0) >> CHUNK_SHIFT
    qc = lax.broadcasted_iota(jnp.int32, mask_ref.shape, 1) >> CHUNK_SHIFT
    mask_ref[...] = (kc <= qc).astype(mask_ref.dtype)


def _transpose_to_bf16(x):
    return x.astype(jnp.float32).T.astype(jnp.bfloat16)


def _da_attn_kernel(lq1_ref, lk1_ref, lq2_ref, lk2_ref, sg_ref, q_ref, k_ref, v_ref,
                    o_ref, vt_ref, wq_ref, mask_ref, acc_ref, *, lambda_init):
    tq, tk, e = ATTN_TQ, ATTN_TK, 2 * DA_HEAD_DIM
    halves = tq // ATTN_G
    _fill_tri_mask(mask_ref)

    def fill_vt(ki):
        k0 = pl.multiple_of(ki * tk, tk)
        vt_ref[ki, :e] = _transpose_to_bf16(v_ref[pl.ds(k0, tk), :])
        vt_ref[ki, e:] = jnp.ones((SUM_ROWS, tk), vt_ref.dtype)

    lam = (jnp.exp(jnp.sum(lq1_ref[...] * lk1_ref[...], keepdims=True))
           - jnp.exp(jnp.sum(lq2_ref[...] * lk2_ref[...], keepdims=True))
           + lambda_init)
    groups = tuple((comp, half) for comp in range(2) for half in range(halves))

    def q_body(qi, carry):
        q0 = pl.multiple_of(qi * tq, tq)
        qt = q_ref[pl.ds(q0, tq), :].astype(jnp.float32).T
        row = lax.broadcasted_iota(jnp.int32, qt.shape, 0)
        wq_ref[0] = jnp.where(row < DA_HEAD_DIM, qt, 0.0).astype(wq_ref.dtype)
        wq_ref[1] = jnp.where(row >= DA_HEAD_DIM, qt, 0.0).astype(wq_ref.dtype)

        def finish_half(half):
            a0, a1 = acc_ref[half], acc_ref[halves + half]
            o = a0[:e] * (1.0 / a0[e:e + 1]) - lam * (a1[:e] * (1.0 / a1[e:e + 1]))
            o = _rms(o.T, sg_ref[...]) * (1.0 - lambda_init)
            r0 = pl.multiple_of(q0 + half * ATTN_G, ATTN_G)
            o_ref[pl.ds(r0, ATTN_G), :] = o.astype(o_ref.dtype)

        _attn_q_tile(
            qi, groups, mask_ref, acc_ref,
            k_tile=lambda ki, comp: k_ref[pl.ds(pl.multiple_of(ki * tk, tk), tk), :],
            wq_group=lambda comp, half: wq_ref[comp, :, half * ATTN_G:(half + 1) * ATTN_G],
            vt_tile=lambda ki, comp: vt_ref[ki], fill_vt=fill_vt, finish_half=finish_half)
        return carry

    lax.fori_loop(0, q_ref.shape[0] // tq, q_body, 0)


def _da_attention(qkv, lq1, lk1, lq2, lk2, subln_g, lambda_init):
    b, s, _ = qkv.shape
    tq, tk = ATTN_TQ, ATTN_TK
    hd = 2 * DA_HEAD_DIM
    vec = pl.BlockSpec((1, DA_HEAD_DIM), lambda bi, h: (0, 0))
    return pl.pallas_call(
        functools.partial(_da_attn_kernel, lambda_init=lambda_init),
        grid=(b, DA_HEADS),
        in_specs=[
            vec, vec, vec, vec,
            pl.BlockSpec((1, hd), lambda bi, h: (0, 0)),
            pl.BlockSpec((None, s, hd), lambda bi, h: (bi, 0, h)),
            pl.BlockSpec((None, s, hd), lambda bi, h: (bi, 0, DA_HEADS + h)),
            pl.BlockSpec((None, s, hd), lambda bi, h: (bi, 0, 2 * DA_HEADS + h)),
        ],
        out_specs=pl.BlockSpec((None, s, hd), lambda bi, h: (bi, 0, h)),
        out_shape=jax.ShapeDtypeStruct((b, s, DA_HEADS * hd), jnp.bfloat16),
        scratch_shapes=[
            pltpu.VMEM((s // tk, hd + SUM_ROWS, tk), jnp.bfloat16),
            pltpu.VMEM((2, hd, tq), jnp.bfloat16),
            pltpu.VMEM((tk, ATTN_G), jnp.float32),
            pltpu.VMEM((2 * tq // ATTN_G, hd + SUM_ROWS, ATTN_G), jnp.float32),
        ],
        compiler_params=_compiler_params(("parallel", "parallel")),
        name="da_attn",
    )(lq1, lk1, lq2, lk2, subln_g, qkv, qkv, qkv)


def _mla_proj_kernel(x_ref, g_ref, win_ref, gq_ref, gkv_ref, wuq_ref, wuk_ref, wuv_ref,
                     cq_ref, aq_ref, bq_ref, ck_ref, ak_ref, bk_ref,
                     q_ref, k_ref, v_ref):
    h = _rms(x_ref[...], g_ref[...]).astype(jnp.bfloat16)
    proj = _dot(h, win_ref[...])
    kv_lo = MLA_Q_RANK
    pe_lo = MLA_Q_RANK + MLA_KV_RANK
    c_q = _rms(proj[:, :kv_lo], gq_ref[...]).astype(jnp.bfloat16)
    c_kv = _rms(proj[:, kv_lo:pe_lo], gkv_ref[...]).astype(jnp.bfloat16)
    half = MLA_ROPE // 2
    k_pe = _rope_block(proj[:, pe_lo:pe_lo + LANES], ck_ref[...], ak_ref[...], bk_ref[...], half)
    cq, aq, bq = cq_ref[...], aq_ref[...], bq_ref[...]
    q = _dot(c_q, wuq_ref[...])
    k = _dot(c_kv, wuk_ref[...])
    for hd in range(MLA_HEADS):
        sl = slice(hd * LANES, (hd + 1) * LANES)
        q_ref[:, sl] = _rope_block(q[:, sl], cq, aq, bq, half).astype(q_ref.dtype)
        k_ref[:, sl] = (k[:, sl] + k_pe).astype(k_ref.dtype)
    v_ref[...] = _dot(c_kv, wuv_ref[...]).astype(v_ref.dtype)


def _mla_proj(x2d, g, w_in, g_q, g_kv, w_uq, w_uk, w_uv, q_tables, k_tables, seq_len):
    t = x2d.shape[0]
    tm = ROW_TILE
    pos_blocks = seq_len // tm
    tab_spec = pl.BlockSpec((tm, LANES), lambda i: (i % pos_blocks, 0))

    def whole(arr):
        return pl.BlockSpec(arr.shape, lambda i: (0, 0))

    qk_cols = MLA_HEADS * LANES
    v_cols = MLA_HEADS * MLA_V
    return pl.pallas_call(
        _mla_proj_kernel,
        grid=(t // tm,),
        in_specs=[pl.BlockSpec((tm, D_MODEL), lambda i: (i, 0)),
                  whole(g), whole(w_in), whole(g_q), whole(g_kv),
                  whole(w_uq), whole(w_uk), whole(w_uv)] + [tab_spec] * 6,
        out_specs=[pl.BlockSpec((tm, qk_cols), lambda i: (i, 0)),
                   pl.BlockSpec((tm, qk_cols), lambda i: (i, 0)),
                   pl.BlockSpec((tm, v_cols), lambda i: (i, 0))],
        out_shape=[jax.ShapeDtypeStruct((t, qk_cols), jnp.bfloat16),
                   jax.ShapeDtypeStruct((t, qk_cols), jnp.bfloat16),
                   jax.ShapeDtypeStruct((t, v_cols), jnp.bfloat16)],
        compiler_params=_compiler_params(("parallel",)),
        name="mla_proj",
    )(x2d, g, w_in, g_q, g_kv, w_uq, w_uk, w_uv, *q_tables, *k_tables)


def _mla_attn_kernel(q_ref, k_ref, v_ref, o_ref, vt_ref, wq_ref, mask_ref, acc_ref):
    tq, tk, e = ATTN_TQ, ATTN_TK, MLA_V
    halves = tq // ATTN_G
    _fill_tri_mask(mask_ref)

    def fill_vt(ki):
        k0 = pl.multiple_of(ki * tk, tk)
        vt = _transpose_to_bf16(v_ref[pl.ds(k0, tk), :])
        for head in range(2):
            vt_ref[ki, head, :e] = vt[head * e:(head + 1) * e]
            vt_ref[ki, head, e:] = jnp.ones((SUM_ROWS, tk), vt_ref.dtype)

    groups = tuple((head, half) for head in range(2) for half in range(halves))

    def q_body(qi, carry):
        q0 = pl.multiple_of(qi * tq, tq)
        for head in range(2):
            wq_ref[head] = _transpose_to_bf16(
                q_ref[pl.ds(q0, tq), head * LANES:(head + 1) * LANES])

        def finish_half(half):
            parts = []
            for head in range(2):
                a = acc_ref[head * halves + half]
                parts.append(a[:e] * (1.0 / a[e:e + 1]))
            r0 = pl.multiple_of(q0 + half * ATTN_G, ATTN_G)
            o_ref[pl.ds(r0, ATTN_G), :] = jnp.concatenate(parts, axis=0).T.astype(o_ref.dtype)

        _attn_q_tile(
            qi, groups, mask_ref, acc_ref,
            k_tile=lambda ki, head: k_ref[pl.ds(pl.multiple_of(ki * tk, tk), tk),
                                          head * LANES:(head + 1) * LANES],
            wq_group=lambda head, half: wq_ref[head, :, half * ATTN_G:(half + 1) * ATTN_G],
            vt_tile=lambda ki, head: vt_ref[ki, head],
            fill_vt=fill_vt, finish_half=finish_half)
        return carry

    lax.fori_loop(0, q_ref.shape[0] // tq, q_body, 0)


def _mla_attention(q, k, v):
    b, s, _ = q.shape
    tq, tk = ATTN_TQ, ATTN_TK
    pairs = MLA_HEADS // 2
    return pl.pallas_call(
        _mla_attn_kernel,
        grid=(b, pairs),
        in_specs=[
            pl.BlockSpec((None, s, 2 * LANES), lambda bi, hp: (bi, 0, hp)),
            pl.BlockSpec((None, s, 2 * LANES), lambda bi, hp: (bi, 0, hp)),
            pl.BlockSpec((None, s, LANES), lambda bi, hp: (bi, 0, hp)),
        ],
        out_specs=pl.BlockSpec((None, s, LANES), lambda bi, hp: (bi, 0, hp)),
        out_shape=jax.ShapeDtypeStruct((b, s, MLA_HEADS * MLA_V), jnp.bfloat16),
        scratch_shapes=[
            pltpu.VMEM((s // tk, 2, MLA_V + SUM_ROWS, tk), jnp.bfloat16),
            pltpu.VMEM((2, LANES, tq), jnp.bfloat16),
            pltpu.VMEM((tk, ATTN_G), jnp.float32),
            pltpu.VMEM((2 * tq // ATTN_G, MLA_V + SUM_ROWS, ATTN_G), jnp.float32),
        ],
        compiler_params=_compiler_params(("parallel", "parallel")),
        name="mla_attn",
    )(q, k, v)


def _out_proj_kernel(x_ref, o_ref, w_ref, y_ref):
    y_ref[...] = x_ref[...] + _dot(o_ref[...], w_ref[...])


def _out_proj(x2d, o2d, w_o):
    t = x2d.shape[0]
    tm = ROW_TILE
    return pl.pallas_call(
        _out_proj_kernel,
        grid=(t // tm,),
        in_specs=[pl.BlockSpec((tm, D_MODEL), lambda i: (i, 0)),
                  pl.BlockSpec((tm, D_MODEL), lambda i: (i, 0)),
                  pl.BlockSpec((D_MODEL, D_MODEL), lambda i: (0, 0))],
        out_specs=pl.BlockSpec((tm, D_MODEL), lambda i: (i, 0)),
        out_shape=jax.ShapeDtypeStruct((t, D_MODEL), jnp.float32),
        compiler_params=_compiler_params(("parallel",)),
        name="out_proj",
    )(x2d, o2d, w_o)


def _ffn_kernel(xh_ref, x_ref, g_ref, wup_ref, cw_ref, wdn_ref, gf_ref, y_ref,
                h_ref, u_ref, acc_ref, *, pos_blocks, final_norm):
    i = pl.program_id(0)
    g = g_ref[...]
    x = x_ref[...]
    halo = jnp.where(i % pos_blocks != 0, _rms(xh_ref[...], g), 0.0)
    h_ref[:HALO] = halo.astype(h_ref.dtype)
    h_ref[HALO:] = _rms(x, g).astype(h_ref.dtype)
    acc_ref[...] = jnp.zeros(acc_ref.shape, jnp.float32)
    tm = x.shape[0]
    nf = wup_ref.shape[0]

    def up(f, slot):
        u_ref[slot] = _dot(h_ref[...], wup_ref[f])

    def down(f, slot):
        u = u_ref.at[slot]
        cw = cw_ref[f]
        c = (u[pl.ds(HALO, tm), :] * cw[2:3] + u[pl.ds(HALO - 1, tm), :] * cw[1:2]
             + u[pl.ds(HALO - 2, tm), :] * cw[0:1] + cw[3:4])
        gate, val = c[:, :FFN_CHUNK], c[:, FFN_CHUNK:]
        act = gate / (1.0 + jnp.exp(-gate)) * val
        acc_ref[...] += _dot(act.astype(jnp.bfloat16), wdn_ref[f])

    ring = u_ref.shape[0]

    def ring_walk(j, carry):
        for r in range(ring):
            up(ring * j + r + 1, (r + 1) % ring)
            down(ring * j + r, r)
        return carry

    up(0, 0)
    looped = (nf - 1) // ring * ring
    lax.fori_loop(0, looped // ring, ring_walk, 0)
    for f in range(looped, nf):
        if f + 1 < nf:
            up(f + 1, (f + 1) % ring)
        down(f, f % ring)
    y = x + acc_ref[...]
    if final_norm:
        y = _rms(y, gf_ref[...])
    y_ref[...] = y


def _ffn(x2d, g, w_up, cw, w_down, g_final, seq_len, final_norm):
    t = x2d.shape[0]
    tm = ROW_TILE
    pos_blocks = seq_len // tm
    halo_per_tile = tm // HALO

    def whole(arr):
        nd = arr.ndim
        return pl.BlockSpec(arr.shape, lambda i: (0,) * nd)

    return pl.pallas_call(
        functools.partial(_ffn_kernel, pos_blocks=pos_blocks, final_norm=final_norm),
        grid=(t // tm,),
        in_specs=[
            pl.BlockSpec((HALO, D_MODEL),
                         lambda i: (jnp.maximum(i * halo_per_tile - 1, 0), 0)),
            pl.BlockSpec((tm, D_MODEL), lambda i: (i, 0)),
            whole(g), whole(w_up), whole(cw), whole(w_down), whole(g_final),
        ],
        out_specs=pl.BlockSpec((tm, D_MODEL), lambda i: (i, 0)),
        out_shape=jax.ShapeDtypeStruct((t, D_MODEL), jnp.float32),
        scratch_shapes=[
            pltpu.VMEM((HALO + tm, D_MODEL), jnp.bfloat16),
            pltpu.VMEM((FFN_RING, HALO + tm, 2 * FFN_CHUNK), jnp.float32),
            pltpu.VMEM((tm, D_MODEL), jnp.float32),
        ],
        compiler_params=_compiler_params(("parallel",)),
        name="conv_ffn",
    )(x2d, x2d, g, w_up, cw, w_down, g_final)


def _ffn_weights(w_up, conv_w, conv_b, w_down):
    nf = FFN_HIDDEN // FFN_CHUNK

    def chunks(a):
        lead = a.shape[:-1]
        a = a.reshape(lead + (2, nf, FFN_CHUNK))
        a = jnp.moveaxis(a, -2, 0)
        return a.reshape((nf,) + lead + (2 * FFN_CHUNK,))

    w_up_c = chunks(w_up).astype(jnp.bfloat16)
    taps = chunks(conv_w)
    bias = chunks(conv_b[None])
    cw = jnp.concatenate([taps, bias, jnp.zeros((nf, 4, 2 * FFN_CHUNK), jnp.float32)], axis=1)
    w_dn_c = w_down.reshape(nf, FFN_CHUNK, D_MODEL).astype(jnp.bfloat16)
    return w_up_c, cw, w_dn_c


def kernel(x, attn_norm_g, ffn_norm_g, da_w_qkv, da_lam_q1, da_lam_k1, da_lam_q2, da_lam_k2,
           da_subln_g, da_w_o, mla_w_in, mla_q_norm_g, mla_kv_norm_g, mla_w_uq, mla_w_ukv,
           mla_w_o, ffn_w_up, ffn_conv_w, ffn_conv_b, ffn_w_down, final_norm_g):
    b, s, d = x.shape
    assert d == D_MODEL and s % ROW_TILE == 0 and s % ATTN_TQ == 0 and ATTN_TQ % ATTN_TK == 0
    t = b * s
    bf16 = jnp.bfloat16
    x2d = x.reshape(t, d)
    g_final = final_norm_g.reshape(1, d)

    da_tables = _rope_tables(s, DA_ROT, 0, DA_HEAD_DIM)
    mla_k_tables = _rope_tables(s, MLA_ROPE, MLA_NOPE, LANES)
    mla_scale = (MLA_NOPE + MLA_ROPE) ** -0.5 * LOG2_E
    mla_q_tables = tuple(tb * mla_scale for tb in mla_k_tables)

    for i in range(DEPTH):
        j = i // N_MIXERS
        g_attn = attn_norm_g[i].reshape(1, d)
        if i % N_MIXERS == 0:
            lambda_init = 0.8 - 0.6 * float(np.exp(-0.3 * i))
            qkv = _da_proj(x2d, g_attn, da_w_qkv[j].astype(bf16), da_tables, s)
            o = _da_attention(qkv.reshape(b, s, 3 * d),
                              da_lam_q1[j].reshape(1, -1), da_lam_k1[j].reshape(1, -1),
                              da_lam_q2[j].reshape(1, -1), da_lam_k2[j].reshape(1, -1),
                              da_subln_g[j].reshape(1, -1), lambda_init)
            w_o = da_w_o[j]
        else:
            pe_pad = jnp.zeros((d, LANES), jnp.float32)
            pe_lo = MLA_Q_RANK + MLA_KV_RANK
            pe_pad = pe_pad.at[:, MLA_NOPE:MLA_NOPE + MLA_ROPE].set(mla_w_in[j][:, pe_lo:])
            w_in = jnp.concatenate([mla_w_in[j][:, :pe_lo], pe_pad], axis=1).astype(bf16)
            qk_dim = MLA_NOPE + MLA_ROPE
            w_uq = mla_w_uq[j].reshape(MLA_Q_RANK, MLA_HEADS, qk_dim)
            w_uq = jnp.pad(w_uq, ((0, 0), (0, 0), (0, LANES - qk_dim)))
            w_uq = w_uq.reshape(MLA_Q_RANK, MLA_HEADS * LANES).astype(bf16)
            w_ukv = mla_w_ukv[j].reshape(MLA_KV_RANK, MLA_HEADS, MLA_NOPE + MLA_V)
            w_uk = jnp.pad(w_ukv[:, :, :MLA_NOPE], ((0, 0), (0, 0), (0, LANES - MLA_NOPE)))
            w_uk = w_uk.reshape(MLA_KV_RANK, MLA_HEADS * LANES).astype(bf16)
            w_uv = w_ukv[:, :, MLA_NOPE:].reshape(MLA_KV_RANK, MLA_HEADS * MLA_V).astype(bf16)
            q, k, v = _mla_proj(x2d, g_attn, w_in, mla_q_norm_g[j].reshape(1, -1),
                                mla_kv_norm_g[j].reshape(1, -1), w_uq, w_uk, w_uv,
                                mla_q_tables, mla_k_tables, s)
            o = _mla_attention(q.reshape(b, s, -1), k.reshape(b, s, -1), v.reshape(b, s, -1))
            w_o = mla_w_o[j]
        x2d = _out_proj(x2d, o.reshape(t, d), w_o.astype(bf16))
        w_up_c, cw, w_dn_c = _ffn_weights(ffn_w_up[i], ffn_conv_w[i], ffn_conv_b[i],
                                          ffn_w_down[i])
        x2d = _ffn(x2d, ffn_norm_g[i].reshape(1, d), w_up_c, cw, w_dn_c, g_final, s,
                   final_norm=(i == DEPTH - 1))
    return x2d.reshape(b, s, d)
```

```python
import functools

import numpy as np
import jax
import jax.numpy as jnp
from jax import lax
from jax.experimental import pallas as pl
from jax.experimental.pallas import tpu as pltpu

D_MODEL = 1024
DEPTH = 2
CHUNK = 64
ROPE_THETA = 500000.0
NORM_EPS = 1e-6
N_MIXERS = 2

DA_HEADS = 8
DA_HEAD_DIM = D_MODEL // (2 * DA_HEADS)
DA_ROT = DA_HEAD_DIM // 4

MLA_HEADS = 16
MLA_Q_RANK = 384
MLA_KV_RANK = 256
MLA_NOPE = 64
MLA_ROPE = 32
MLA_V = 64

FFN_HIDDEN = 2816
CONV_WIDTH = 3

LANES = 128
BF16_SUBLANES = 16
MASK_VALUE = -1e30

ROW_TILE = 512
ATTN_TQ = 1024
ATTN_TILES_PER_ITER = 4
ATTN_TK = 256
ATTN_G = 256
SUM_ROWS = BF16_SUBLANES
ATTN_LOOKAHEAD = 8
CHUNK_SHIFT = CHUNK.bit_length() - 1
LOG2_E = 1.4426950408889634
FFN_CHUNK = 256
FFN_RING = 3
HALO = BF16_SUBLANES

VMEM_LIMIT = 56 * 1024 * 1024


def _compiler_params(semantics):
    return pltpu.CompilerParams(dimension_semantics=semantics,
                                vmem_limit_bytes=VMEM_LIMIT)


def _rms(x, g):
    return x * lax.rsqrt(jnp.mean(x * x, axis=-1, keepdims=True) + NORM_EPS) * g


def _dot(a, b):
    return jnp.dot(a, b, preferred_element_type=jnp.float32)


def _rope_block(x, c, a, bm, half):
    return (x * c + pltpu.roll(x, half, axis=1) * a
            + pltpu.roll(x, LANES - half, axis=1) * bm)


def _rope_tables(seq_len, rot_dim, lane_start, period):
    half = rot_dim // 2
    inv = ROPE_THETA ** (-jnp.arange(0, rot_dim, 2, dtype=jnp.float32) / rot_dim)
    ang = jnp.arange(seq_len, dtype=jnp.float32)[:, None] * inv[None, :]
    cos, sin = jnp.cos(ang), jnp.sin(ang)
    r = np.arange(LANES) % period - lane_start
    first = (r >= 0) & (r < half)
    second = (r >= half) & (r < rot_dim)
    idx = np.where(first, r, np.where(second, r - half, 0))
    cos_l, sin_l = cos[:, idx], sin[:, idx]
    c = jnp.where(first | second, cos_l, 1.0)
    a = jnp.where(second, sin_l, 0.0)
    bm = jnp.where(first, -sin_l, 0.0)
    return c, a, bm


def _da_proj_kernel(x_ref, g_ref, w_ref, c_ref, a_ref, bm_ref, o_ref):
    h = _rms(x_ref[...], g_ref[...]).astype(jnp.bfloat16)
    c, a, bm = c_ref[...], a_ref[...], bm_ref[...]
    scale = DA_HEAD_DIM ** -0.5 * LOG2_E
    for part in range(3):
        lo = part * D_MODEL
        y = _dot(h, w_ref[:, lo:lo + D_MODEL])
        for blk in range(D_MODEL // LANES):
            yb = y[:, blk * LANES:(blk + 1) * LANES]
            if part < 2:
                yb = _rope_block(yb, c, a, bm, DA_ROT // 2)
            if part == 0:
                yb = yb * scale
            o_ref[:, lo + blk * LANES:lo + (blk + 1) * LANES] = yb.astype(o_ref.dtype)


def _da_proj(x2d, g, w_qkv, tables, seq_len):
    t = x2d.shape[0]
    tm = ROW_TILE
    pos_blocks = seq_len // tm
    tab_spec = pl.BlockSpec((tm, LANES), lambda i: (i % pos_blocks, 0))
    return pl.pallas_call(
        _da_proj_kernel,
        grid=(t // tm,),
        in_specs=[
            pl.BlockSpec((tm, D_MODEL), lambda i: (i, 0)),
            pl.BlockSpec((1, D_MODEL), lambda i: (0, 0)),
            pl.BlockSpec((D_MODEL, 3 * D_MODEL), lambda i: (0, 0)),
            tab_spec, tab_spec, tab_spec,
        ],
        out_specs=pl.BlockSpec((tm, 3 * D_MODEL), lambda i: (i, 0)),
        out_shape=jax.ShapeDtypeStruct((t, 3 * D_MODEL), jnp.bfloat16),
        compiler_params=_compiler_params(("parallel",)),
        name="da_proj",
    )(x2d, g, w_qkv, *tables)


def _softmax_pv(s, vt1, m_prev, acc_ref):
    m_new = jnp.maximum(m_prev, jnp.max(s, axis=0, keepdims=True))
    alpha = jnp.exp2(m_prev - m_new)
    p = jnp.exp2(s - m_new).astype(vt1.dtype)
    acc_ref[...] = alpha * acc_ref[...] + _dot(vt1, p)
    return m_new


def _attn_q_tile(qi, groups, mask_ref, acc_ref, k_tile, wq_group, vt_tile, fill_vt,
                 finish_half):
    halves = ATTN_TQ // ATTN_G
    assert ATTN_TK == ATTN_G and halves % ATTN_TILES_PER_ITER == 0

    def scores(ki, slot, masked):
        stream, half = groups[slot]
        s = _dot(k_tile(ki, stream), wq_group(stream, half))
        if masked:
            s = jnp.where(mask_ref[...] > 0.0, s, MASK_VALUE)
        return s

    def run(items, ms, after=None):
        ms = list(ms)
        pending = [scores(*item) for item in items[:ATTN_LOOKAHEAD]]
        for n, (ki, slot, _) in enumerate(items):
            s = pending.pop(0)
            if n + ATTN_LOOKAHEAD < len(items):
                pending.append(scores(*items[n + ATTN_LOOKAHEAD]))
            ms[slot] = _softmax_pv(s, vt_tile(ki, groups[slot][0]), ms[slot], acc_ref.at[slot])
            if after and n in after:
                after[n]()
        return tuple(ms)

    def full_tiles(j, ms):
        return run([(ATTN_TILES_PER_ITER * j + t, slot, False)
                    for t in range(ATTN_TILES_PER_ITER) for slot in range(len(groups))], ms)

    acc_ref[...] = jnp.zeros(acc_ref.shape, jnp.float32)
    ms = tuple(jnp.full((1, ATTN_G), MASK_VALUE, jnp.float32) for _ in groups)
    ms = lax.fori_loop(0, qi * (halves // ATTN_TILES_PER_ITER), full_tiles, ms)
    diag, after = [], {}
    for d in range(halves):
        fill_vt(halves * qi + d)
        diag += [(halves * qi + d, slot, half == d)
                 for slot, (_, half) in enumerate(groups) if half >= d]
        after[len(diag) - 1] = functools.partial(finish_half, d)
    run(diag, ms, after)


def _fill_tri_mask(mask_ref):
    kc = lax.broadcasted_iota(jnp.int32, mask_ref.shape, 0) >> CHUNK_SHIFT
    qc = lax.broadcasted_iota(jnp.int32, mask_ref.shape, 1) >> CHUNK_SHIFT
    mask_ref[...] = (kc <= qc).astype(mask_ref.dtype)


def _transpose_to_bf16(x):
    return x.T


def _da_attn_kernel(lq1_ref, lk1_ref, lq2_ref, lk2_ref, sg_ref, q_ref, k_ref, v_ref,
                    o_ref, vt_ref, wq_ref, mask_ref, acc_ref, *, lambda_init):
    tq, tk, e = ATTN_TQ, ATTN_TK, 2 * DA_HEAD_DIM
    halves = tq // ATTN_G
    _fill_tri_mask(mask_ref)

    def fill_vt(ki):
        k0 = pl.multiple_of(ki * tk, tk)
        vt_ref[ki, :e] = _transpose_to_bf16(v_ref[pl.ds(k0, tk), :])
        vt_ref[ki, e:] = jnp.ones((SUM_ROWS, tk), vt_ref.dtype)

    lam = (jnp.exp(jnp.sum(lq1_ref[...] * lk1_ref[...], keepdims=True))
           - jnp.exp(jnp.sum(lq2_ref[...] * lk2_ref[...], keepdims=True))
           + lambda_init)
    groups = tuple((comp, half) for comp in range(2) for half in range(halves))

    def q_body(qi, carry):
        q0 = pl.multiple_of(qi * tq, tq)
        qt = q_ref[pl.ds(q0, tq), :].T
        row = lax.broadcasted_iota(jnp.int32, qt.shape, 0)
        zero = jnp.zeros_like(qt)
        wq_ref[0] = jnp.where(row < DA_HEAD_DIM, qt, zero)
        wq_ref[1] = jnp.where(row >= DA_HEAD_DIM, qt, zero)

        def finish_half(half):
            a0, a1 = acc_ref[half], acc_ref[halves + half]
            o = a0[:e] * (1.0 / a0[e:e + 1]) - lam * (a1[:e] * (1.0 / a1[e:e + 1]))
            inv = lax.rsqrt(jnp.mean(o * o, axis=0, keepdims=True) + NORM_EPS)
            o = o * inv * sg_ref[...] * (1.0 - lambda_init)
            r0 = pl.multiple_of(q0 + half * ATTN_G, ATTN_G)
            o_ref[pl.ds(r0, ATTN_G), :] = o.astype(o_ref.dtype).T

        _attn_q_tile(
            qi, groups, mask_ref, acc_ref,
            k_tile=lambda ki, comp: k_ref[pl.ds(pl.multiple_of(ki * tk, tk), tk), :],
            wq_group=lambda comp, half: wq_ref[comp, :, half * ATTN_G:(half + 1) * ATTN_G],
            vt_tile=lambda ki, comp: vt_ref[ki], fill_vt=fill_vt, finish_half=finish_half)
        return carry

    lax.fori_loop(0, q_ref.shape[0] // tq, q_body, 0)


def _da_attention(qkv, lq1, lk1, lq2, lk2, subln_g, lambda_init):
    b, s, _ = qkv.shape
    tq, tk = ATTN_TQ, ATTN_TK
    hd = 2 * DA_HEAD_DIM
    vec = pl.BlockSpec((1, DA_HEAD_DIM), lambda bi, h: (0, 0))
    return pl.pallas_call(
        functools.partial(_da_attn_kernel, lambda_init=lambda_init),
        grid=(b, DA_HEADS),
        in_specs=[
            vec, vec, vec, vec,
            pl.BlockSpec((hd, ATTN_G), lambda bi, h: (0, 0)),
            pl.BlockSpec((None, s, hd), lambda bi, h: (bi, 0, h)),
            pl.BlockSpec((None, s, hd), lambda bi, h: (bi, 0, DA_HEADS + h)),
            pl.BlockSpec((None, s, hd), lambda bi, h: (bi, 0, 2 * DA_HEADS + h)),
        ],
        out_specs=pl.BlockSpec((None, s, hd), lambda bi, h: (bi, 0, h)),
        out_shape=jax.ShapeDtypeStruct((b, s, DA_HEADS * hd), jnp.bfloat16),
        scratch_shapes=[
            pltpu.VMEM((s // tk, hd + SUM_ROWS, tk), jnp.bfloat16),
            pltpu.VMEM((2, hd, tq), jnp.bfloat16),
            pltpu.VMEM((tk, ATTN_G), jnp.float32),
            pltpu.VMEM((2 * tq // ATTN_G, hd + SUM_ROWS, ATTN_G), jnp.float32),
        ],
        compiler_params=_compiler_params(("parallel", "parallel")),
        name="da_attn",
    )(lq1, lk1, lq2, lk2, subln_g, qkv, qkv, qkv)


def _mla_proj_kernel(x_ref, g_ref, win_ref, gq_ref, gkv_ref, wuq_ref, wuk_ref, wuv_ref,
                     cq_ref, aq_ref, bq_ref, ck_ref, ak_ref, bk_ref,
                     q_ref, k_ref, v_ref):
    h = _rms(x_ref[...], g_ref[...]).astype(jnp.bfloat16)
    proj = _dot(h, win_ref[...])
    kv_lo = MLA_Q_RANK
    pe_lo = MLA_Q_RANK + MLA_KV_RANK
    c_q = _rms(proj[:, :kv_lo], gq_ref[...]).astype(jnp.bfloat16)
    c_kv = _rms(proj[:, kv_lo:pe_lo], gkv_ref[...]).astype(jnp.bfloat16)
    half = MLA_ROPE // 2
    k_pe = _rope_block(proj[:, pe_lo:pe_lo + LANES], ck_ref[...], ak_ref[...], bk_ref[...], half)
    cq, aq, bq = cq_ref[...], aq_ref[...], bq_ref[...]
    q = _dot(c_q, wuq_ref[...])
    k = _dot(c_kv, wuk_ref[...])
    for hd in range(MLA_HEADS):
        sl = slice(hd * LANES, (hd + 1) * LANES)
        q_ref[:, sl] = _rope_block(q[:, sl], cq, aq, bq, half).astype(q_ref.dtype)
        k_ref[:, sl] = (k[:, sl] + k_pe).astype(k_ref.dtype)
    v_ref[...] = _dot(c_kv, wuv_ref[...]).astype(v_ref.dtype)


def _mla_proj(x2d, g, w_in, g_q, g_kv, w_uq, w_uk, w_uv, q_tables, k_tables, seq_len):
    t = x2d.shape[0]
    tm = ROW_TILE
    pos_blocks = seq_len // tm
    tab_spec = pl.BlockSpec((tm, LANES), lambda i: (i % pos_blocks, 0))

    def whole(arr):
        return pl.BlockSpec(arr.shape, lambda i: (0, 0))

    qk_cols = MLA_HEADS * LANES
    v_cols = MLA_HEADS * MLA_V
    return pl.pallas_call(
        _mla_proj_kernel,
        grid=(t // tm,),
        in_specs=[pl.BlockSpec((tm, D_MODEL), lambda i: (i, 0)),
                  whole(g), whole(w_in), whole(g_q), whole(g_kv),
                  whole(w_uq), whole(w_uk), whole(w_uv)] + [tab_spec] * 6,
        out_specs=[pl.BlockSpec((tm, qk_cols), lambda i: (i, 0)),
                   pl.BlockSpec((tm, qk_cols), lambda i: (i, 0)),
                   pl.BlockSpec((tm, v_cols), lambda i: (i, 0))],
        out_shape=[jax.ShapeDtypeStruct((t, qk_cols), jnp.bfloat16),
                   jax.ShapeDtypeStruct((t, qk_cols), jnp.bfloat16),
                   jax.ShapeDtypeStruct((t, v_cols), jnp.bfloat16)],
        compiler_params=_compiler_params(("parallel",)),
        name="mla_proj",
    )(x2d, g, w_in, g_q, g_kv, w_uq, w_uk, w_uv, *q_tables, *k_tables)


def _mla_attn_kernel(q_ref, k_ref, v_ref, o_ref, vt_ref, wq_ref, mask_ref, acc_ref):
    tq, tk, e = ATTN_TQ, ATTN_TK, MLA_V
    halves = tq // ATTN_G
    _fill_tri_mask(mask_ref)

    def fill_vt(ki):
        k0 = pl.multiple_of(ki * tk, tk)
        vt = _transpose_to_bf16(v_ref[pl.ds(k0, tk), :])
        for head in range(2):
            vt_ref[ki, head, :e] = vt[head * e:(head + 1) * e]
            vt_ref[ki, head, e:] = jnp.ones((SUM_ROWS, tk), vt_ref.dtype)

    groups = tuple((head, half) for head in range(2) for half in range(halves))

    def q_body(qi, carry):
        q0 = pl.multiple_of(qi * tq, tq)
        for head in range(2):
            wq_ref[head] = _transpose_to_bf16(
                q_ref[pl.ds(q0, tq), head * LANES:(head + 1) * LANES])

        def finish_half(half):
            parts = []
            for head in range(2):
                a = acc_ref[head * halves + half]
                parts.append(a[:e] * (1.0 / a[e:e + 1]))
            r0 = pl.multiple_of(q0 + half * ATTN_G, ATTN_G)
            o_ref[pl.ds(r0, ATTN_G), :] = jnp.concatenate(parts, axis=0).astype(o_ref.dtype).T

        _attn_q_tile(
            qi, groups, mask_ref, acc_ref,
            k_tile=lambda ki, head: k_ref[pl.ds(pl.multiple_of(ki * tk, tk), tk),
                                          head * LANES:(head + 1) * LANES],
            wq_group=lambda head, half: wq_ref[head, :, half * ATTN_G:(half + 1) * ATTN_G],
            vt_tile=lambda ki, head: vt_ref[ki, head],
            fill_vt=fill_vt, finish_half=finish_half)
        return carry

    lax.fori_loop(0, q_ref.shape[0] // tq, q_body, 0)


def _mla_attention(q, k, v):
    b, s, _ = q.shape
    tq, tk = ATTN_TQ, ATTN_TK
    pairs = MLA_HEADS // 2
    return pl.pallas_call(
        _mla_attn_kernel,
        grid=(b, pairs),
        in_specs=[
            pl.BlockSpec((None, s, 2 * LANES), lambda bi, hp: (bi, 0, hp)),
            pl.BlockSpec((None, s, 2 * LANES), lambda bi, hp: (bi, 0, hp)),
            pl.BlockSpec((None, s, LANES), lambda bi, hp: (bi, 0, hp)),
        ],
        out_specs=pl.BlockSpec((None, s, LANES), lambda bi, hp: (bi, 0, hp)),
        out_shape=jax.ShapeDtypeStruct((b, s, MLA_HEADS * MLA_V), jnp.bfloat16),
        scratch_shapes=[
            pltpu.VMEM((s // tk, 2, MLA_V + SUM_ROWS, tk), jnp.bfloat16),
            pltpu.VMEM((2, LANES, tq), jnp.bfloat16),
            pltpu.VMEM((tk, ATTN_G), jnp.float32),
            pltpu.VMEM((2 * tq // ATTN_G, MLA_V + SUM_ROWS, ATTN_G), jnp.float32),
        ],
        compiler_params=_compiler_params(("parallel", "parallel")),
        name="mla_attn",
    )(q, k, v)


def _ffn_kernel(xh_ref, x_ref, oh_ref, o_ref, wo_ref, g_ref, wup_ref, cw_ref, wdn_ref, gf_ref,
                y_ref, h_ref, u_ref, acc_ref, *, pos_blocks, final_norm):
    i = pl.program_id(0)
    g = g_ref[...]
    o_all = jnp.concatenate([oh_ref[...], o_ref[...]], axis=0)
    x_all = jnp.concatenate([xh_ref[...], x_ref[...]], axis=0) + _dot(o_all, wo_ref[...])
    x = x_all[HALO:]
    h_all = _rms(x_all, g)
    h_ref[:HALO] = jnp.where(i % pos_blocks != 0, h_all[:HALO], 0.0).astype(h_ref.dtype)
    h_ref[HALO:] = h_all[HALO:].astype(h_ref.dtype)
    acc_ref[...] = jnp.zeros(acc_ref.shape, jnp.float32)
    tm = x.shape[0]
    nf = wdn_ref.shape[0]

    def up(f, slot):
        h = h_ref[...]
        for part in range(2):
            col = part * FFN_HIDDEN + f * FFN_CHUNK
            if not isinstance(col, int):
                col = pl.multiple_of(col, LANES)
            u_ref[slot, :, part * FFN_CHUNK:(part + 1) * FFN_CHUNK] = _dot(
                h, wup_ref[:, pl.ds(col, FFN_CHUNK)])

    def act(f, slot):
        u = u_ref.at[slot]
        cw = cw_ref[f]
        c = (u[pl.ds(HALO, tm), :] * cw[2:3] + u[pl.ds(HALO - 1, tm), :] * cw[1:2]
             + u[pl.ds(HALO - 2, tm), :] * cw[0:1] + cw[3:4])
        gate, val = c[:, :FFN_CHUNK], c[:, FFN_CHUNK:]
        return (gate / (1.0 + jnp.exp(-gate)) * val).astype(jnp.bfloat16)

    def step(f, slot, prefetch):
        if prefetch:
            up(f + 1, (slot + 1) % ring)
        acc_ref[...] += _dot(act(f, slot), wdn_ref[f])

    ring = u_ref.shape[0]

    def ring_walk(j, carry):
        for r in range(ring):
            step(ring * j + r, r, True)
        return carry

    up(0, 0)
    looped = (nf - 1) // ring * ring
    lax.fori_loop(0, looped // ring, ring_walk, 0)
    for f in range(looped, nf):
        step(f, f % ring, f + 1 < nf)
    y = x + acc_ref[...]
    if final_norm:
        y = _rms(y, gf_ref[...])
    y_ref[...] = y


def _ffn(x2d, o2d, w_o, g, w_up, cw, w_down, g_final, seq_len, final_norm):
    t = x2d.shape[0]
    tm = ROW_TILE
    pos_blocks = seq_len // tm
    halo_per_tile = tm // HALO

    def whole(arr):
        nd = arr.ndim
        return pl.BlockSpec(arr.shape, lambda i: (0,) * nd, pipeline_mode=pl.Buffered(1))

    halo_spec = pl.BlockSpec((HALO, D_MODEL),
                             lambda i: (jnp.maximum(i * halo_per_tile - 1, 0), 0))
    main_spec = pl.BlockSpec((tm, D_MODEL), lambda i: (i, 0))
    return pl.pallas_call(
        functools.partial(_ffn_kernel, pos_blocks=pos_blocks, final_norm=final_norm),
        grid=(t // tm,),
        in_specs=[
            halo_spec, main_spec, halo_spec, main_spec, whole(w_o),
            whole(g), whole(w_up), whole(cw), whole(w_down), whole(g_final),
        ],
        out_specs=pl.BlockSpec((tm, D_MODEL), lambda i: (i, 0)),
        out_shape=jax.ShapeDtypeStruct((t, D_MODEL), jnp.float32),
        scratch_shapes=[
            pltpu.VMEM((HALO + tm, D_MODEL), jnp.bfloat16),
            pltpu.VMEM((FFN_RING, HALO + tm, 2 * FFN_CHUNK), jnp.float32),
            pltpu.VMEM((tm, D_MODEL), jnp.float32),
        ],
        compiler_params=_compiler_params(("parallel",)),
        name="conv_ffn",
    )(x2d, x2d, o2d, o2d, w_o, g, w_up, cw, w_down, g_final)


def _ffn_weights(w_up, conv_w, conv_b, w_down):
    nf = FFN_HIDDEN // FFN_CHUNK

    def chunks(a):
        lead = a.shape[:-1]
        a = a.reshape(lead + (2, nf, FFN_CHUNK))
        a = jnp.moveaxis(a, -2, 0)
        return a.reshape((nf,) + lead + (2 * FFN_CHUNK,))

    w_up_c = w_up.astype(jnp.bfloat16)
    taps = chunks(conv_w)
    bias = chunks(conv_b[None])
    cw = jnp.concatenate([taps, bias, jnp.zeros((nf, 4, 2 * FFN_CHUNK), jnp.float32)], axis=1)
    w_dn_c = w_down.reshape(nf, FFN_CHUNK, D_MODEL).astype(jnp.bfloat16)
    return w_up_c, cw, w_dn_c


def kernel(x, attn_norm_g, ffn_norm_g, da_w_qkv, da_lam_q1, da_lam_k1, da_lam_q2, da_lam_k2,
           da_subln_g, da_w_o, mla_w_in, mla_q_norm_g, mla_kv_norm_g, mla_w_uq, mla_w_ukv,
           mla_w_o, ffn_w_up, ffn_conv_w, ffn_conv_b, ffn_w_down, final_norm_g):
    b, s, d = x.shape
    assert d == D_MODEL and s % ROW_TILE == 0 and s % ATTN_TQ == 0 and ATTN_TQ % ATTN_TK == 0
    t = b * s
    bf16 = jnp.bfloat16
    x2d = x.reshape(t, d)
    g_final = final_norm_g.reshape(1, d)

    da_tables = _rope_tables(s, DA_ROT, 0, DA_HEAD_DIM)
    mla_k_tables = _rope_tables(s, MLA_ROPE, MLA_NOPE, LANES)
    mla_scale = (MLA_NOPE + MLA_ROPE) ** -0.5 * LOG2_E
    mla_q_tables = tuple(tb * mla_scale for tb in mla_k_tables)

    for i in range(DEPTH):
        j = i // N_MIXERS
        g_attn = attn_norm_g[i].reshape(1, d)
        if i % N_MIXERS == 0:
            lambda_init = 0.8 - 0.6 * float(np.exp(-0.3 * i))
            qkv = _da_proj(x2d, g_attn, da_w_qkv[j].astype(bf16), da_tables, s)
            o = _da_attention(qkv.reshape(b, s, 3 * d),
                              da_lam_q1[j].reshape(1, -1), da_lam_k1[j].reshape(1, -1),
                              da_lam_q2[j].reshape(1, -1), da_lam_k2[j].reshape(1, -1),
                              jnp.broadcast_to(da_subln_g[j][:, None], (2 * DA_HEAD_DIM, ATTN_G)),
                              lambda_init)
            w_o = da_w_o[j]
        else:
            pe_pad = jnp.zeros((d, LANES), jnp.float32)
            pe_lo = MLA_Q_RANK + MLA_KV_RANK
            pe_pad = pe_pad.at[:, MLA_NOPE:MLA_NOPE + MLA_ROPE].set(mla_w_in[j][:, pe_lo:])
            w_in = jnp.concatenate([mla_w_in[j][:, :pe_lo], pe_pad], axis=1).astype(bf16)
            qk_dim = MLA_NOPE + MLA_ROPE
            w_uq = mla_w_uq[j].reshape(MLA_Q_RANK, MLA_HEADS, qk_dim)
            w_uq = jnp.pad(w_uq, ((0, 0), (0, 0), (0, LANES - qk_dim)))
            w_uq = w_uq.reshape(MLA_Q_RANK, MLA_HEADS * LANES).astype(bf16)
            w_ukv = mla_w_ukv[j].reshape(MLA_KV_RANK, MLA_HEADS, MLA_NOPE + MLA_V)
            w_uk = jnp.pad(w_ukv[:, :, :MLA_NOPE], ((0, 0), (0, 0), (0, LANES - MLA_NOPE)))
            w_uk = w_uk.reshape(MLA_KV_RANK, MLA_HEADS * LANES).astype(bf16)
            w_uv = w_ukv[:, :, MLA_NOPE:].reshape(MLA_KV_RANK, MLA_HEADS * MLA_V).astype(bf16)
            q, k, v = _mla_proj(x2d, g_attn, w_in, mla_q_norm_g[j].reshape(1, -1),
                                mla_kv_norm_g[j].reshape(1, -1), w_uq, w_uk, w_uv,
                                mla_q_tables, mla_k_tables, s)
            o = _mla_attention(q.reshape(b, s, -1), k.reshape(b, s, -1), v.reshape(b, s, -1))
            w_o = mla_w_o[j]
        w_up_c, cw, w_dn_c = _ffn_weights(ffn_w_up[i], ffn_conv_w[i], ffn_conv_b[i],
                                          ffn_w_down[i])
        x2d = _ffn(x2d, o.reshape(t, d), w_o.astype(bf16), ffn_norm_g[i].reshape(1, d),
                   w_up_c, cw, w_dn_c, g_final, s, final_norm=(i == DEPTH - 1))
    return x2d.reshape(b, s, d)
```

```python
import functools

import numpy as np
import jax
import jax.numpy as jnp
from jax import lax
from jax.experimental import pallas as pl
from jax.experimental.pallas import tpu as pltpu

D_MODEL = 1024
DEPTH = 2
CHUNK = 64
ROPE_THETA = 500000.0
NORM_EPS = 1e-6
N_MIXERS = 2

DA_HEADS = 8
DA_HEAD_DIM = D_MODEL // (2 * DA_HEADS)
DA_ROT = DA_HEAD_DIM // 4

MLA_HEADS = 16
MLA_Q_RANK = 384
MLA_KV_RANK = 256
MLA_NOPE = 64
MLA_ROPE = 32
MLA_V = 64

FFN_HIDDEN = 2816
CONV_WIDTH = 3

LANES = 128
BF16_SUBLANES = 16
MASK_VALUE = -1e30

ROW_TILE = 512
ATTN_TQ = 1024
ATTN_TILES_PER_ITER = 4
ATTN_TK = 256
ATTN_G = 256
SUM_ROWS = BF16_SUBLANES
ATTN_LOOKAHEAD = 8
CHUNK_SHIFT = CHUNK.bit_length() - 1
LOG2_E = 1.4426950408889634
FFN_CHUNK = 256
FFN_RING = 4
FFN_LOOKAHEAD = 3
HALO = BF16_SUBLANES

VMEM_LIMIT = 56 * 1024 * 1024


def _compiler_params(semantics):
    return pltpu.CompilerParams(dimension_semantics=semantics,
                                vmem_limit_bytes=VMEM_LIMIT)


def _rms(x, g):
    return x * lax.rsqrt(jnp.mean(x * x, axis=-1, keepdims=True) + NORM_EPS) * g


def _dot(a, b):
    return jnp.dot(a, b, preferred_element_type=jnp.float32)


def _rope_block(x, c, a, bm, half):
    return (x * c + pltpu.roll(x, half, axis=1) * a
            + pltpu.roll(x, LANES - half, axis=1) * bm)


def _rope_tables(seq_len, rot_dim, lane_start, period):
    half = rot_dim // 2
    inv = ROPE_THETA ** (-jnp.arange(0, rot_dim, 2, dtype=jnp.float32) / rot_dim)
    ang = jnp.arange(seq_len, dtype=jnp.float32)[:, None] * inv[None, :]
    cos, sin = jnp.cos(ang), jnp.sin(ang)
    r = np.arange(LANES) % period - lane_start
    first = (r >= 0) & (r < half)
    second = (r >= half) & (r < rot_dim)
    idx = np.where(first, r, np.where(second, r - half, 0))
    cos_l, sin_l = cos[:, idx], sin[:, idx]
    c = jnp.where(first | second, cos_l, 1.0)
    a = jnp.where(second, sin_l, 0.0)
    bm = jnp.where(first, -sin_l, 0.0)
    return c, a, bm


def _da_proj_kernel(x_ref, g_ref, w_ref, c_ref, a_ref, bm_ref, o_ref):
    h = _rms(x_ref[...], g_ref[...]).astype(jnp.bfloat16)
    c, a, bm = c_ref[...], a_ref[...], bm_ref[...]
    scale = DA_HEAD_DIM ** -0.5 * LOG2_E
    for part in range(3):
        lo = part * D_MODEL
        y = _dot(h, w_ref[:, lo:lo + D_MODEL])
        for blk in range(D_MODEL // LANES):
            yb = y[:, blk * LANES:(blk + 1) * LANES]
            if part < 2:
                yb = _rope_block(yb, c, a, bm, DA_ROT // 2)
            if part == 0:
                yb = yb * scale
            o_ref[:, lo + blk * LANES:lo + (blk + 1) * LANES] = yb.astype(o_ref.dtype)


def _da_proj(x2d, g, w_qkv, tables, seq_len):
    t = x2d.shape[0]
    tm = ROW_TILE
    pos_blocks = seq_len // tm
    tab_spec = pl.BlockSpec((tm, LANES), lambda i: (i % pos_blocks, 0))
    return pl.pallas_call(
        _da_proj_kernel,
        grid=(t // tm,),
        in_specs=[
            pl.BlockSpec((tm, D_MODEL), lambda i: (i, 0)),
            pl.BlockSpec((1, D_MODEL), lambda i: (0, 0)),
            pl.BlockSpec((D_MODEL, 3 * D_MODEL), lambda i: (0, 0)),
            tab_spec, tab_spec, tab_spec,
        ],
        out_specs=pl.BlockSpec((tm, 3 * D_MODEL), lambda i: (i, 0)),
        out_shape=jax.ShapeDtypeStruct((t, 3 * D_MODEL), jnp.bfloat16),
        compiler_params=_compiler_params(("parallel",)),
        name="da_proj",
    )(x2d, g, w_qkv, *tables)


def _softmax_pv(s, vt1, m_prev, acc_ref):
    m_new = jnp.maximum(m_prev, jnp.max(s, axis=0, keepdims=True))
    alpha = jnp.exp2(m_prev - m_new)
    p = jnp.exp2(s - m_new).astype(vt1.dtype)
    acc_ref[...] = alpha * acc_ref[...] + _dot(vt1, p)
    return m_new


def _attn_q_tile(qi, groups, mask_ref, acc_ref, k_tile, wq_group, vt_tile, fill_vt,
                 finish_half):
    halves = ATTN_TQ // ATTN_G
    assert ATTN_TK == ATTN_G and halves % ATTN_TILES_PER_ITER == 0

    def scores(ki, slot, masked):
        stream, half = groups[slot]
        s = _dot(k_tile(ki, stream), wq_group(stream, half))
        if masked:
            s = jnp.where(mask_ref[...] > 0.0, s, MASK_VALUE)
        return s

    def run(items, ms, after=None):
        ms = list(ms)
        pending = [scores(*item) for item in items[:ATTN_LOOKAHEAD]]
        for n, (ki, slot, _) in enumerate(items):
            s = pending.pop(0)
            if n + ATTN_LOOKAHEAD < len(items):
                pending.append(scores(*items[n + ATTN_LOOKAHEAD]))
            ms[slot] = _softmax_pv(s, vt_tile(ki, groups[slot][0]), ms[slot], acc_ref.at[slot])
            if after and n in after:
                after[n]()
        return tuple(ms)

    def full_items(j):
        return [(ATTN_TILES_PER_ITER * j + t, slot, False)
                for t in range(ATTN_TILES_PER_ITER) for slot in range(len(groups))]

    def diagonal(ms, lead):
        items, after = list(lead), {}
        for d in range(halves):
            fill_vt(halves * qi + d)
            items += [(halves * qi + d, slot, half == d)
                      for slot, (_, half) in enumerate(groups) if half >= d]
            after[len(items) - 1] = functools.partial(finish_half, d)
        run(items, ms, after)

    acc_ref[...] = jnp.zeros(acc_ref.shape, jnp.float32)
    ms = tuple(jnp.full((1, ATTN_G), MASK_VALUE, jnp.float32) for _ in groups)
    trips = qi * (halves // ATTN_TILES_PER_ITER)
    ms = lax.fori_loop(0, jnp.maximum(trips - 1, 0), lambda j, c: run(full_items(j), c), ms)

    @pl.when(trips == 0)
    def _():
        diagonal(ms, [])

    @pl.when(trips > 0)
    def _():
        diagonal(ms, full_items(trips - 1))


def _fill_tri_mask(mask_ref):
    kc = lax.broadcasted_iota(jnp.int32, mask_ref.shape, 0) >> CHUNK_SHIFT
    qc = lax.broadcasted_iota(jnp.int32, mask_ref.shape, 1) >> CHUNK_SHIFT
    mask_ref[...] = (kc <= qc).astype(mask_ref.dtype)


def _transpose_to_bf16(x):
    return x.T


def _da_attn_kernel(lq1_ref, lk1_ref, lq2_ref, lk2_ref, sg_ref, q_ref, k_ref, v_ref,
                    o_ref, vt_ref, wq_ref, mask_ref, acc_ref, *, lambda_init):
    tq, tk, e = ATTN_TQ, ATTN_TK, 2 * DA_HEAD_DIM
    halves = tq // ATTN_G
    _fill_tri_mask(mask_ref)

    def fill_vt(ki):
        k0 = pl.multiple_of(ki * tk, tk)
        vt_ref[ki, :e] = _transpose_to_bf16(v_ref[pl.ds(k0, tk), :])
        vt_ref[ki, e:] = jnp.ones((SUM_ROWS, tk), vt_ref.dtype)

    lam = (jnp.exp(jnp.sum(lq1_ref[...] * lk1_ref[...], keepdims=True))
           - jnp.exp(jnp.sum(lq2_ref[...] * lk2_ref[...], keepdims=True))
           + lambda_init)
    groups = tuple((comp, half) for comp in range(2) for half in range(halves))

    def q_body(qi, carry):
        q0 = pl.multiple_of(qi * tq, tq)
        qt = q_ref[pl.ds(q0, tq), :].T
        row = lax.broadcasted_iota(jnp.int32, qt.shape, 0)
        zero = jnp.zeros_like(qt)
        wq_ref[0] = jnp.where(row < DA_HEAD_DIM, qt, zero)
        wq_ref[1] = jnp.where(row >= DA_HEAD_DIM, qt, zero)

        def finish_half(half):
            a0, a1 = acc_ref[half], acc_ref[halves + half]
            o = a0[:e] * (1.0 / a0[e:e + 1]) - lam * (a1[:e] * (1.0 / a1[e:e + 1]))
            inv = lax.rsqrt(jnp.mean(o * o, axis=0, keepdims=True) + NORM_EPS)
            o = o * inv * sg_ref[...] * (1.0 - lambda_init)
            r0 = pl.multiple_of(q0 + half * ATTN_G, ATTN_G)
            o_ref[pl.ds(r0, ATTN_G), :] = o.astype(o_ref.dtype).T

        _attn_q_tile(
            qi, groups, mask_ref, acc_ref,
            k_tile=lambda ki, comp: k_ref[pl.ds(pl.multiple_of(ki * tk, tk), tk), :],
            wq_group=lambda comp, half: wq_ref[comp, :, half * ATTN_G:(half + 1) * ATTN_G],
            vt_tile=lambda ki, comp: vt_ref[ki], fill_vt=fill_vt, finish_half=finish_half)
        return carry

    lax.fori_loop(0, q_ref.shape[0] // tq, q_body, 0)


def _da_attention(qkv, lq1, lk1, lq2, lk2, subln_g, lambda_init):
    b, s, _ = qkv.shape
    tq, tk = ATTN_TQ, ATTN_TK
    hd = 2 * DA_HEAD_DIM
    vec = pl.BlockSpec((1, DA_HEAD_DIM), lambda bi, h: (0, 0))
    return pl.pallas_call(
        functools.partial(_da_attn_kernel, lambda_init=lambda_init),
        grid=(b, DA_HEADS),
        in_specs=[
            vec, vec, vec, vec,
            pl.BlockSpec((hd, ATTN_G), lambda bi, h: (0, 0)),
            pl.BlockSpec((None, s, hd), lambda bi, h: (bi, 0, h)),
            pl.BlockSpec((None, s, hd), lambda bi, h: (bi, 0, DA_HEADS + h)),
            pl.BlockSpec((None, s, hd), lambda bi, h: (bi, 0, 2 * DA_HEADS + h)),
        ],
        out_specs=pl.BlockSpec((None, s, hd), lambda bi, h: (bi, 0, h)),
        out_shape=jax.ShapeDtypeStruct((b, s, DA_HEADS * hd), jnp.bfloat16),
        scratch_shapes=[
            pltpu.VMEM((s // tk, hd + SUM_ROWS, tk), jnp.bfloat16),
            pltpu.VMEM((2, hd, tq), jnp.bfloat16),
            pltpu.VMEM((tk, ATTN_G), jnp.float32),
            pltpu.VMEM((2 * tq // ATTN_G, hd + SUM_ROWS, ATTN_G), jnp.float32),
        ],
        compiler_params=_compiler_params(("parallel", "parallel")),
        name="da_attn",
    )(lq1, lk1, lq2, lk2, subln_g, qkv, qkv, qkv)


def _mla_proj_kernel(x_ref, g_ref, win_ref, gq_ref, gkv_ref, wuq_ref, wuk_ref, wuv_ref,
                     cq_ref, aq_ref, bq_ref, ck_ref, ak_ref, bk_ref,
                     q_ref, k_ref, v_ref):
    kv_lo = MLA_Q_RANK
    pe_lo = MLA_Q_RANK + MLA_KV_RANK
    half = MLA_ROPE // 2
    tm = x_ref.shape[0]
    for rows in (pl.ds(0, tm // 2), pl.ds(tm // 2, tm // 2)):
        h = _rms(x_ref[rows, :], g_ref[...]).astype(jnp.bfloat16)
        proj = _dot(h, win_ref[...])
        c_q = _rms(proj[:, :kv_lo], gq_ref[...]).astype(jnp.bfloat16)
        c_kv = _rms(proj[:, kv_lo:pe_lo], gkv_ref[...]).astype(jnp.bfloat16)
        k_pe = _rope_block(proj[:, pe_lo:pe_lo + LANES], ck_ref[rows, :], ak_ref[rows, :],
                           bk_ref[rows, :], half)
        cq, aq, bq = cq_ref[rows, :], aq_ref[rows, :], bq_ref[rows, :]
        q = _dot(c_q, wuq_ref[...])
        k = _dot(c_kv, wuk_ref[...])
        for hd in range(MLA_HEADS):
            sl = slice(hd * LANES, (hd + 1) * LANES)
            q_ref[rows, sl] = _rope_block(q[:, sl], cq, aq, bq, half).astype(q_ref.dtype)
            k_ref[rows, sl] = (k[:, sl] + k_pe).astype(k_ref.dtype)
        v_ref[rows, :] = _dot(c_kv, wuv_ref[...]).astype(v_ref.dtype)


def _mla_proj(x2d, g, w_in, g_q, g_kv, w_uq, w_uk, w_uv, q_tables, k_tables, seq_len):
    t = x2d.shape[0]
    tm = ROW_TILE
    pos_blocks = seq_len // tm
    tab_spec = pl.BlockSpec((tm, LANES), lambda i: (i % pos_blocks, 0))

    def whole(arr):
        return pl.BlockSpec(arr.shape, lambda i: (0, 0))

    qk_cols = MLA_HEADS * LANES
    v_cols = MLA_HEADS * MLA_V
    return pl.pallas_call(
        _mla_proj_kernel,
        grid=(t // tm,),
        in_specs=[pl.BlockSpec((tm, D_MODEL), lambda i: (i, 0)),
                  whole(g), whole(w_in), whole(g_q), whole(g_kv),
                  whole(w_uq), whole(w_uk), whole(w_uv)] + [tab_spec] * 6,
        out_specs=[pl.BlockSpec((tm, qk_cols), lambda i: (i, 0)),
                   pl.BlockSpec((tm, qk_cols), lambda i: (i, 0)),
                   pl.BlockSpec((tm, v_cols), lambda i: (i, 0))],
        out_shape=[jax.ShapeDtypeStruct((t, qk_cols), jnp.bfloat16),
                   jax.ShapeDtypeStruct((t, qk_cols), jnp.bfloat16),
                   jax.ShapeDtypeStruct((t, v_cols), jnp.bfloat16)],
        compiler_params=_compiler_params(("parallel",)),
        name="mla_proj",
    )(x2d, g, w_in, g_q, g_kv, w_uq, w_uk, w_uv, *q_tables, *k_tables)


def _mla_attn_kernel(q_ref, k_ref, v_ref, o_ref, vt_ref, wq_ref, mask_ref, acc_ref):
    tq, tk, e = ATTN_TQ, ATTN_TK, MLA_V
    halves = tq // ATTN_G
    _fill_tri_mask(mask_ref)

    def fill_vt(ki):
        k0 = pl.multiple_of(ki * tk, tk)
        vt = _transpose_to_bf16(v_ref[pl.ds(k0, tk), :])
        for head in range(2):
            vt_ref[ki, head, :e] = vt[head * e:(head + 1) * e]
            vt_ref[ki, head, e:] = jnp.ones((SUM_ROWS, tk), vt_ref.dtype)

    groups = tuple((head, half) for head in range(2) for half in range(halves))

    def q_body(qi, carry):
        q0 = pl.multiple_of(qi * tq, tq)
        for head in range(2):
            wq_ref[head] = _transpose_to_bf16(
                q_ref[pl.ds(q0, tq), head * LANES:(head + 1) * LANES])

        def finish_half(half):
            parts = []
            for head in range(2):
                a = acc_ref[head * halves + half]
                parts.append(a[:e] * (1.0 / a[e:e + 1]))
            r0 = pl.multiple_of(q0 + half * ATTN_G, ATTN_G)
            o_ref[pl.ds(r0, ATTN_G), :] = jnp.concatenate(parts, axis=0).astype(o_ref.dtype).T

        _attn_q_tile(
            qi, groups, mask_ref, acc_ref,
            k_tile=lambda ki, head: k_ref[pl.ds(pl.multiple_of(ki * tk, tk), tk),
                                          head * LANES:(head + 1) * LANES],
            wq_group=lambda head, half: wq_ref[head, :, half * ATTN_G:(half + 1) * ATTN_G],
            vt_tile=lambda ki, head: vt_ref[ki, head],
            fill_vt=fill_vt, finish_half=finish_half)
        return carry

    lax.fori_loop(0, q_ref.shape[0] // tq, q_body, 0)


def _mla_attention(q, k, v):
    b, s, _ = q.shape
    tq, tk = ATTN_TQ, ATTN_TK
    pairs = MLA_HEADS // 2
    return pl.pallas_call(
        _mla_attn_kernel,
        grid=(b, pairs),
        in_specs=[
            pl.BlockSpec((None, s, 2 * LANES), lambda bi, hp: (bi, 0, hp)),
            pl.BlockSpec((None, s, 2 * LANES), lambda bi, hp: (bi, 0, hp)),
            pl.BlockSpec((None, s, LANES), lambda bi, hp: (bi, 0, hp)),
        ],
        out_specs=pl.BlockSpec((None, s, LANES), lambda bi, hp: (bi, 0, hp)),
        out_shape=jax.ShapeDtypeStruct((b, s, MLA_HEADS * MLA_V), jnp.bfloat16),
        scratch_shapes=[
            pltpu.VMEM((s // tk, 2, MLA_V + SUM_ROWS, tk), jnp.bfloat16),
            pltpu.VMEM((2, LANES, tq), jnp.bfloat16),
            pltpu.VMEM((tk, ATTN_G), jnp.float32),
            pltpu.VMEM((2 * tq // ATTN_G, MLA_V + SUM_ROWS, ATTN_G), jnp.float32),
        ],
        compiler_params=_compiler_params(("parallel", "parallel")),
        name="mla_attn",
    )(q, k, v)


def _ffn_kernel(xh_ref, x_ref, oh_ref, o_ref, wo_ref, g_ref, wup_ref, cw_ref, wdn_ref, gf_ref,
                y_ref, h_ref, u_ref, acc_ref, *, pos_blocks, final_norm):
    i = pl.program_id(0)
    g = g_ref[...]
    o_all = jnp.concatenate([oh_ref[...], o_ref[...]], axis=0)
    x_all = jnp.concatenate([xh_ref[...], x_ref[...]], axis=0) + _dot(o_all, wo_ref[...])
    x = x_all[HALO:]
    h_all = _rms(x_all, g)
    h_ref[:HALO] = jnp.where(i % pos_blocks != 0, h_all[:HALO], 0.0).astype(h_ref.dtype)
    h_ref[HALO:] = h_all[HALO:].astype(h_ref.dtype)
    acc_ref[...] = jnp.zeros(acc_ref.shape, jnp.float32)
    tm = x.shape[0]
    nf = wdn_ref.shape[0]

    def up(f, slot):
        h = h_ref[...]
        for part in range(2):
            col = part * FFN_HIDDEN + f * FFN_CHUNK
            if not isinstance(col, int):
                col = pl.multiple_of(col, LANES)
            u_ref[slot, :, part * FFN_CHUNK:(part + 1) * FFN_CHUNK] = _dot(
                h, wup_ref[:, pl.ds(col, FFN_CHUNK)])

    def act(f, slot):
        u = u_ref.at[slot]
        cw = cw_ref[f]
        c = (u[pl.ds(HALO, tm), :] * cw[2:3] + u[pl.ds(HALO - 1, tm), :] * cw[1:2]
             + u[pl.ds(HALO - 2, tm), :] * cw[0:1] + cw[3:4])
        gate, val = c[:, :FFN_CHUNK], c[:, FFN_CHUNK:]
        return (gate / (1.0 + jnp.exp(-gate)) * val).astype(jnp.bfloat16)

    ring = u_ref.shape[0]
    assert ring >= FFN_LOOKAHEAD + 1

    def block(f0, n_chunks, n_prefetch):
        for k in range(1, min(FFN_LOOKAHEAD, n_prefetch) + 1):
            up(f0 + k, k % ring)
        for k in range(n_chunks):
            a = act(f0 + k, k % ring)
            acc_ref[...] += _dot(a, wdn_ref[f0 + k])
            ahead = k + FFN_LOOKAHEAD + 1
            if ahead <= n_prefetch:
                up(f0 + ahead, ahead % ring)

    def ring_walk(j, carry):
        block(ring * j, ring, ring)
        return carry

    up(0, 0)
    looped = (nf - 1) // ring * ring
    lax.fori_loop(0, looped // ring, ring_walk, 0)
    block(looped, nf - looped, nf - looped - 1)
    y = x + acc_ref[...]
    if final_norm:
        y = _rms(y, gf_ref[...])
    y_ref[...] = y


def _ffn(x2d, o2d, w_o, g, w_up, cw, w_down, g_final, seq_len, final_norm):
    t = x2d.shape[0]
    tm = ROW_TILE
    pos_blocks = seq_len // tm
    halo_per_tile = tm // HALO

    def whole(arr):
        nd = arr.ndim
        return pl.BlockSpec(arr.shape, lambda i: (0,) * nd, pipeline_mode=pl.Buffered(1))

    halo_spec = pl.BlockSpec((HALO, D_MODEL),
                             lambda i: (jnp.maximum(i * halo_per_tile - 1, 0), 0))
    main_spec = pl.BlockSpec((tm, D_MODEL), lambda i: (i, 0))
    return pl.pallas_call(
        functools.partial(_ffn_kernel, pos_blocks=pos_blocks, final_norm=final_norm),
        grid=(t // tm,),
        in_specs=[
            halo_spec, main_spec, halo_spec, main_spec, whole(w_o),
            whole(g), whole(w_up), whole(cw), whole(w_down), whole(g_final),
        ],
        out_specs=pl.BlockSpec((tm, D_MODEL), lambda i: (i, 0)),
        out_shape=jax.ShapeDtypeStruct((t, D_MODEL), jnp.float32),
        scratch_shapes=[
            pltpu.VMEM((HALO + tm, D_MODEL), jnp.bfloat16),
            pltpu.VMEM((FFN_RING, HALO + tm, 2 * FFN_CHUNK), jnp.float32),
            pltpu.VMEM((tm, D_MODEL), jnp.float32),
        ],
        compiler_params=_compiler_params(("parallel",)),
        name="conv_ffn",
    )(x2d, x2d, o2d, o2d, w_o, g, w_up, cw, w_down, g_final)


def _ffn_weights(w_up, conv_w, conv_b, w_down):
    nf = FFN_HIDDEN // FFN_CHUNK

    def chunks(a):
        lead = a.shape[:-1]
        a = a.reshape(lead + (2, nf, FFN_CHUNK))
        a = jnp.moveaxis(a, -2, 0)
        return a.reshape((nf,) + lead + (2 * FFN_CHUNK,))

    w_up_c = w_up.astype(jnp.bfloat16)
    taps = chunks(conv_w)
    bias = chunks(conv_b[None])
    cw = jnp.concatenate([taps, bias, jnp.zeros((nf, 4, 2 * FFN_CHUNK), jnp.float32)], axis=1)
    w_dn_c = w_down.reshape(nf, FFN_CHUNK, D_MODEL).astype(jnp.bfloat16)
    return w_up_c, cw, w_dn_c


def kernel(x, attn_norm_g, ffn_norm_g, da_w_qkv, da_lam_q1, da_lam_k1, da_lam_q2, da_lam_k2,
           da_subln_g, da_w_o, mla_w_in, mla_q_norm_g, mla_kv_norm_g, mla_w_uq, mla_w_ukv,
           mla_w_o, ffn_w_up, ffn_conv_w, ffn_conv_b, ffn_w_down, final_norm_g):
    b, s, d = x.shape
    assert d == D_MODEL and s % ROW_TILE == 0 and s % ATTN_TQ == 0 and ATTN_TQ % ATTN_TK == 0
    t = b * s
    bf16 = jnp.bfloat16
    x2d = x.reshape(t, d)
    g_final = final_norm_g.reshape(1, d)

    da_tables = _rope_tables(s, DA_ROT, 0, DA_HEAD_DIM)
    mla_k_tables = _rope_tables(s, MLA_ROPE, MLA_NOPE, LANES)
    mla_scale = (MLA_NOPE + MLA_ROPE) ** -0.5 * LOG2_E
    mla_q_tables = tuple(tb * mla_scale for tb in mla_k_tables)

    for i in range(DEPTH):
        j = i // N_MIXERS
        g_attn = attn_norm_g[i].reshape(1, d)
        if i % N_MIXERS == 0:
            lambda_init = 0.8 - 0.6 * float(np.exp(-0.3 * i))
            qkv = _da_proj(x2d, g_attn, da_w_qkv[j].astype(bf16), da_tables, s)
            o = _da_attention(qkv.reshape(b, s, 3 * d),
                              da_lam_q1[j].reshape(1, -1), da_lam_k1[j].reshape(1, -1),
                              da_lam_q2[j].reshape(1, -1), da_lam_k2[j].reshape(1, -1),
                              jnp.broadcast_to(da_subln_g[j][:, None], (2 * DA_HEAD_DIM, ATTN_G)),
                              lambda_init)
            w_o = da_w_o[j]
        else:
            pe_pad = jnp.zeros((d, LANES), jnp.float32)
            pe_lo = MLA_Q_RANK + MLA_KV_RANK
            pe_pad = pe_pad.at[:, MLA_NOPE:MLA_NOPE + MLA_ROPE].set(mla_w_in[j][:, pe_lo:])
            w_in = jnp.concatenate([mla_w_in[j][:, :pe_lo], pe_pad], axis=1).astype(bf16)
            qk_dim = MLA_NOPE + MLA_ROPE
            w_uq = mla_w_uq[j].reshape(MLA_Q_RANK, MLA_HEADS, qk_dim)
            w_uq = jnp.pad(w_uq, ((0, 0), (0, 0), (0, LANES - qk_dim)))
            w_uq = w_uq.reshape(MLA_Q_RANK, MLA_HEADS * LANES).astype(bf16)
            w_ukv = mla_w_ukv[j].reshape(MLA_KV_RANK, MLA_HEADS, MLA_NOPE + MLA_V)
            w_uk = jnp.pad(w_ukv[:, :, :MLA_NOPE], ((0, 0), (0, 0), (0, LANES - MLA_NOPE)))
            w_uk = w_uk.reshape(MLA_KV_RANK, MLA_HEADS * LANES).astype(bf16)
            w_uv = w_ukv[:, :, MLA_NOPE:].reshape(MLA_KV_RANK, MLA_HEADS * MLA_V).astype(bf16)
            q, k, v = _mla_proj(x2d, g_attn, w_in, mla_q_norm_g[j].reshape(1, -1),
                                mla_kv_norm_g[j].reshape(1, -1), w_uq, w_uk, w_uv,
                                mla_q_tables, mla_k_tables, s)
            o = _mla_attention(q.reshape(b, s, -1), k.reshape(b, s, -1), v.reshape(b, s, -1))
            w_o = mla_w_o[j]
        w_up_c, cw, w_dn_c = _ffn_weights(ffn_w_up[i], ffn_conv_w[i], ffn_conv_b[i],
                                          ffn_w_down[i])
        x2d = _ffn(x2d, o.reshape(t, d), w_o.astype(bf16), ffn_norm_g[i].reshape(1, d),
                   w_up_c, cw, w_dn_c, g_final, s, final_norm=(i == DEPTH - 1))
    return x2d.reshape(b, s, d)
```

```python
import functools

import numpy as np
import jax
import jax.numpy as jnp
from jax import lax
from jax.experimental import pallas as pl
from jax.experimental.pallas import tpu as pltpu

D_MODEL = 1024
DEPTH = 2
CHUNK = 64
ROPE_THETA = 500000.0
NORM_EPS = 1e-6
N_MIXERS = 2

DA_HEADS = 8
DA_HEAD_DIM = D_MODEL // (2 * DA_HEADS)
DA_ROT = DA_HEAD_DIM // 4

MLA_HEADS = 16
MLA_Q_RANK = 384
MLA_KV_RANK = 256
MLA_NOPE = 64
MLA_ROPE = 32
MLA_V = 64

FFN_HIDDEN = 2816
CONV_WIDTH = 3

LANES = 128
BF16_SUBLANES = 16
MASK_VALUE = -1e30

ROW_TILE = 512
ATTN_TQ = 1024
ATTN_TILES_PER_ITER = 4
ATTN_TK = 256
ATTN_G = 256
SUM_ROWS = BF16_SUBLANES
ATTN_LOOKAHEAD = 8
CHUNK_SHIFT = CHUNK.bit_length() - 1
LOG2_E = 1.4426950408889634
FFN_CHUNK = 256
FFN_RING = 4
FFN_LOOKAHEAD = 3
HALO = BF16_SUBLANES

VMEM_LIMIT = 56 * 1024 * 1024


def _compiler_params(semantics):
    return pltpu.CompilerParams(dimension_semantics=semantics,
                                vmem_limit_bytes=VMEM_LIMIT)


def _rms(x, g):
    return x * lax.rsqrt(jnp.mean(x * x, axis=-1, keepdims=True) + NORM_EPS) * g


def _dot(a, b):
    return jnp.dot(a, b, preferred_element_type=jnp.float32)


def _rope_block(x, c, a, bm, half):
    return (x * c + pltpu.roll(x, half, axis=1) * a
            + pltpu.roll(x, LANES - half, axis=1) * bm)


def _rope_tables(seq_len, rot_dim, lane_start, period):
    half = rot_dim // 2
    inv = ROPE_THETA ** (-jnp.arange(0, rot_dim, 2, dtype=jnp.float32) / rot_dim)
    ang = jnp.arange(seq_len, dtype=jnp.float32)[:, None] * inv[None, :]
    cos, sin = jnp.cos(ang), jnp.sin(ang)
    r = np.arange(LANES) % period - lane_start
    first = (r >= 0) & (r < half)
    second = (r >= half) & (r < rot_dim)
    idx = np.where(first, r, np.where(second, r - half, 0))
    cos_l, sin_l = cos[:, idx], sin[:, idx]
    c = jnp.where(first | second, cos_l, 1.0)
    a = jnp.where(second, sin_l, 0.0)
    bm = jnp.where(first, -sin_l, 0.0)
    return c, a, bm


def _mla_lane_map():
    half = MLA_ROPE // 2
    nope = np.full(LANES, -1)
    rope = np.full(LANES, -1)
    rope[:half] = half + np.arange(half)
    nope[half:LANES // 2] = np.arange(LANES // 2 - half)
    rope[LANES // 2:LANES // 2 + half] = np.arange(half)
    rest = MLA_NOPE - (LANES // 2 - half)
    nope[LANES // 2 + half:LANES // 2 + half + rest] = LANES // 2 - half + np.arange(rest)
    return nope, rope


def _place_columns(w, src):
    return jnp.where(src >= 0, w[..., np.maximum(src, 0)], 0.0)


def _mla_rope_tables(seq_len):
    half = MLA_ROPE // 2
    inv = ROPE_THETA ** (-jnp.arange(0, MLA_ROPE, 2, dtype=jnp.float32) / MLA_ROPE)
    ang = jnp.arange(seq_len, dtype=jnp.float32)[:, None] * inv[None, :]
    cos, sin = jnp.cos(ang), jnp.sin(ang)
    _, rope = _mla_lane_map()
    first, second = (rope >= 0) & (rope < half), rope >= half
    idx = np.where(rope >= 0, rope % half, 0)
    cos_l, sin_l = cos[:, idx], sin[:, idx]
    c = jnp.where(first | second, cos_l, 1.0)
    sg = jnp.where(first, -sin_l, jnp.where(second, sin_l, 0.0))
    return c, sg


def _rope_block_paired(x, c, sg):
    return x * c + pltpu.roll(x, LANES // 2, axis=1) * sg


def _da_proj_kernel(x_ref, g_ref, w_ref, c_ref, a_ref, bm_ref, o_ref):
    h = _rms(x_ref[...], g_ref[...]).astype(jnp.bfloat16)
    c, a, bm = c_ref[...], a_ref[...], bm_ref[...]
    scale = DA_HEAD_DIM ** -0.5 * LOG2_E
    for part in range(3):
        lo = part * D_MODEL
        y = _dot(h, w_ref[:, lo:lo + D_MODEL])
        for blk in range(D_MODEL // LANES):
            yb = y[:, blk * LANES:(blk + 1) * LANES]
            if part < 2:
                yb = _rope_block(yb, c, a, bm, DA_ROT // 2)
            if part == 0:
                yb = yb * scale
            o_ref[:, lo + blk * LANES:lo + (blk + 1) * LANES] = yb.astype(o_ref.dtype)


def _da_proj(x2d, g, w_qkv, tables, seq_len):
    t = x2d.shape[0]
    tm = ROW_TILE
    pos_blocks = seq_len // tm
    tab_spec = pl.BlockSpec((tm, LANES), lambda i: (i % pos_blocks, 0))
    return pl.pallas_call(
        _da_proj_kernel,
        grid=(t // tm,),
        in_specs=[
            pl.BlockSpec((tm, D_MODEL), lambda i: (i, 0)),
            pl.BlockSpec((1, D_MODEL), lambda i: (0, 0)),
            pl.BlockSpec((D_MODEL, 3 * D_MODEL), lambda i: (0, 0)),
            tab_spec, tab_spec, tab_spec,
        ],
        out_specs=pl.BlockSpec((tm, 3 * D_MODEL), lambda i: (i, 0)),
        out_shape=jax.ShapeDtypeStruct((t, 3 * D_MODEL), jnp.bfloat16),
        compiler_params=_compiler_params(("parallel",)),
        name="da_proj",
    )(x2d, g, w_qkv, *tables)


def _softmax_pv(s, vt1, m_prev, acc_ref):
    m_new = jnp.maximum(m_prev, jnp.max(s, axis=0, keepdims=True))
    alpha = jnp.exp2(m_prev - m_new)
    p = jnp.exp2(s - m_new).astype(vt1.dtype)
    acc_ref[...] = alpha * acc_ref[...] + _dot(vt1, p)
    return m_new


def _attn_q_tile(qi, groups, mask_ref, acc_ref, k_tile, wq_group, vt_tile, fill_vt,
                 finish_half):
    halves = ATTN_TQ // ATTN_G
    assert ATTN_TK == ATTN_G and halves % ATTN_TILES_PER_ITER == 0

    def scores(ki, slot, masked):
        stream, half = groups[slot]
        s = _dot(k_tile(ki, stream), wq_group(stream, half))
        if masked:
            s = jnp.where(mask_ref[...] > 0.0, s, MASK_VALUE)
        return s

    def run(items, ms, after=None):
        ms = list(ms)
        pending = [scores(*item) for item in items[:ATTN_LOOKAHEAD]]
        for n, (ki, slot, _) in enumerate(items):
            s = pending.pop(0)
            if n + ATTN_LOOKAHEAD < len(items):
                pending.append(scores(*items[n + ATTN_LOOKAHEAD]))
            ms[slot] = _softmax_pv(s, vt_tile(ki, groups[slot][0]), ms[slot], acc_ref.at[slot])
            if after and n in after:
                after[n]()
        return tuple(ms)

    def full_items(j):
        return [(ATTN_TILES_PER_ITER * j + t, slot, False)
                for t in range(ATTN_TILES_PER_ITER) for slot in range(len(groups))]

    def diagonal(ms, lead):
        items, after = list(lead), {}
        for d in range(halves):
            fill_vt(halves * qi + d)
            items += [(halves * qi + d, slot, half == d)
                      for slot, (_, half) in enumerate(groups) if half >= d]
            after[len(items) - 1] = functools.partial(finish_half, d)
        run(items, ms, after)

    acc_ref[...] = jnp.zeros(acc_ref.shape, jnp.float32)
    ms = tuple(jnp.full((1, ATTN_G), MASK_VALUE, jnp.float32) for _ in groups)
    trips = qi * (halves // ATTN_TILES_PER_ITER)
    ms = lax.fori_loop(0, jnp.maximum(trips - 1, 0), lambda j, c: run(full_items(j), c), ms)

    @pl.when(trips == 0)
    def _():
        diagonal(ms, [])

    @pl.when(trips > 0)
    def _():
        diagonal(ms, full_items(trips - 1))


def _fill_tri_mask(mask_ref):
    kc = lax.broadcasted_iota(jnp.int32, mask_ref.shape, 0) >> CHUNK_SHIFT
    qc = lax.broadcasted_iota(jnp.int32, mask_ref.shape, 1) >> CHUNK_SHIFT
    mask_ref[...] = (kc <= qc).astype(mask_ref.dtype)


def _transpose_to_bf16(x):
    return x.T


def _da_attn_kernel(lq1_ref, lk1_ref, lq2_ref, lk2_ref, sg_ref, q_ref, k_ref, v_ref,
                    o_ref, vt_ref, wq_ref, mask_ref, acc_ref, *, lambda_init):
    tq, tk, e = ATTN_TQ, ATTN_TK, 2 * DA_HEAD_DIM
    halves = tq // ATTN_G
    _fill_tri_mask(mask_ref)

    def fill_vt(ki):
        k0 = pl.multiple_of(ki * tk, tk)
        vt_ref[ki, :e] = _transpose_to_bf16(v_ref[pl.ds(k0, tk), :])
        vt_ref[ki, e:] = jnp.ones((SUM_ROWS, tk), vt_ref.dtype)

    lam = (jnp.exp(jnp.sum(lq1_ref[...] * lk1_ref[...], keepdims=True))
           - jnp.exp(jnp.sum(lq2_ref[...] * lk2_ref[...], keepdims=True))
           + lambda_init)
    groups = tuple((comp, half) for comp in range(2) for half in range(halves))

    def q_body(qi, carry):
        q0 = pl.multiple_of(qi * tq, tq)
        qt = q_ref[pl.ds(q0, tq), :].T
        row = lax.broadcasted_iota(jnp.int32, qt.shape, 0)
        zero = jnp.zeros_like(qt)
        wq_ref[0] = jnp.where(row < DA_HEAD_DIM, qt, zero)
        wq_ref[1] = jnp.where(row >= DA_HEAD_DIM, qt, zero)

        def finish_half(half):
            a0, a1 = acc_ref[half], acc_ref[halves + half]
            o = a0[:e] * (1.0 / a0[e:e + 1]) - lam * (a1[:e] * (1.0 / a1[e:e + 1]))
            inv = lax.rsqrt(jnp.mean(o * o, axis=0, keepdims=True) + NORM_EPS)
            o = o * inv * sg_ref[...] * (1.0 - lambda_init)
            r0 = pl.multiple_of(q0 + half * ATTN_G, ATTN_G)
            o_ref[pl.ds(r0, ATTN_G), :] = o.astype(o_ref.dtype).T

        _attn_q_tile(
            qi, groups, mask_ref, acc_ref,
            k_tile=lambda ki, comp: k_ref[pl.ds(pl.multiple_of(ki * tk, tk), tk), :],
            wq_group=lambda comp, half: wq_ref[comp, :, half * ATTN_G:(half + 1) * ATTN_G],
            vt_tile=lambda ki, comp: vt_ref[ki], fill_vt=fill_vt, finish_half=finish_half)
        return carry

    lax.fori_loop(0, q_ref.shape[0] // tq, q_body, 0)


def _da_attention(qkv, lq1, lk1, lq2, lk2, subln_g, lambda_init):
    b, s, _ = qkv.shape
    tq, tk = ATTN_TQ, ATTN_TK
    hd = 2 * DA_HEAD_DIM
    vec = pl.BlockSpec((1, DA_HEAD_DIM), lambda bi, h: (0, 0))
    return pl.pallas_call(
        functools.partial(_da_attn_kernel, lambda_init=lambda_init),
        grid=(b, DA_HEADS),
        in_specs=[
            vec, vec, vec, vec,
            pl.BlockSpec((hd, ATTN_G), lambda bi, h: (0, 0)),
            pl.BlockSpec((None, s, hd), lambda bi, h: (bi, 0, h)),
            pl.BlockSpec((None, s, hd), lambda bi, h: (bi, 0, DA_HEADS + h)),
            pl.BlockSpec((None, s, hd), lambda bi, h: (bi, 0, 2 * DA_HEADS + h)),
        ],
        out_specs=pl.BlockSpec((None, s, hd), lambda bi, h: (bi, 0, h)),
        out_shape=jax.ShapeDtypeStruct((b, s, DA_HEADS * hd), jnp.bfloat16),
        scratch_shapes=[
            pltpu.VMEM((s // tk, hd + SUM_ROWS, tk), jnp.bfloat16),
            pltpu.VMEM((2, hd, tq), jnp.bfloat16),
            pltpu.VMEM((tk, ATTN_G), jnp.float32),
            pltpu.VMEM((2 * tq // ATTN_G, hd + SUM_ROWS, ATTN_G), jnp.float32),
        ],
        compiler_params=_compiler_params(("parallel", "parallel")),
        name="da_attn",
    )(lq1, lk1, lq2, lk2, subln_g, qkv, qkv, qkv)


def _mla_proj_kernel(x_ref, g_ref, win_ref, gq_ref, gkv_ref, wuq_ref, wuk_ref, wuv_ref,
                     cq_ref, sq_ref, ck_ref, sk_ref, q_ref, k_ref, v_ref):
    kv_lo = MLA_Q_RANK
    pe_lo = MLA_Q_RANK + MLA_KV_RANK
    tm = x_ref.shape[0]
    for rows in (pl.ds(0, tm // 2), pl.ds(tm // 2, tm // 2)):
        h = _rms(x_ref[rows, :], g_ref[...]).astype(jnp.bfloat16)
        proj = _dot(h, win_ref[...])
        c_q = _rms(proj[:, :kv_lo], gq_ref[...]).astype(jnp.bfloat16)
        c_kv = _rms(proj[:, kv_lo:pe_lo], gkv_ref[...]).astype(jnp.bfloat16)
        k_pe = _rope_block_paired(proj[:, pe_lo:pe_lo + LANES], ck_ref[rows, :], sk_ref[rows, :])
        cq, sq = cq_ref[rows, :], sq_ref[rows, :]
        q = _dot(c_q, wuq_ref[...])
        k = _dot(c_kv, wuk_ref[...])
        for hd in range(MLA_HEADS):
            sl = slice(hd * LANES, (hd + 1) * LANES)
            q_ref[rows, sl] = _rope_block_paired(q[:, sl], cq, sq).astype(q_ref.dtype)
            k_ref[rows, sl] = (k[:, sl] + k_pe).astype(k_ref.dtype)
        v_ref[rows, :] = _dot(c_kv, wuv_ref[...]).astype(v_ref.dtype)


def _mla_proj(x2d, g, w_in, g_q, g_kv, w_uq, w_uk, w_uv, q_tables, k_tables, seq_len):
    t = x2d.shape[0]
    tm = ROW_TILE
    pos_blocks = seq_len // tm
    tab_spec = pl.BlockSpec((tm, LANES), lambda i: (i % pos_blocks, 0))

    def whole(arr):
        return pl.BlockSpec(arr.shape, lambda i: (0, 0))

    qk_cols = MLA_HEADS * LANES
    v_cols = MLA_HEADS * MLA_V
    return pl.pallas_call(
        _mla_proj_kernel,
        grid=(t // tm,),
        in_specs=[pl.BlockSpec((tm, D_MODEL), lambda i: (i, 0)),
                  whole(g), whole(w_in), whole(g_q), whole(g_kv),
                  whole(w_uq), whole(w_uk), whole(w_uv)] + [tab_spec] * 4,
        out_specs=[pl.BlockSpec((tm, qk_cols), lambda i: (i, 0)),
                   pl.BlockSpec((tm, qk_cols), lambda i: (i, 0)),
                   pl.BlockSpec((tm, v_cols), lambda i: (i, 0))],
        out_shape=[jax.ShapeDtypeStruct((t, qk_cols), jnp.bfloat16),
                   jax.ShapeDtypeStruct((t, qk_cols), jnp.bfloat16),
                   jax.ShapeDtypeStruct((t, v_cols), jnp.bfloat16)],
        compiler_params=_compiler_params(("parallel",)),
        name="mla_proj",
    )(x2d, g, w_in, g_q, g_kv, w_uq, w_uk, w_uv, *q_tables, *k_tables)


def _mla_attn_kernel(q_ref, k_ref, v_ref, o_ref, vt_ref, wq_ref, mask_ref, acc_ref):
    tq, tk, e = ATTN_TQ, ATTN_TK, MLA_V
    halves = tq // ATTN_G
    _fill_tri_mask(mask_ref)

    def fill_vt(ki):
        k0 = pl.multiple_of(ki * tk, tk)
        vt = _transpose_to_bf16(v_ref[pl.ds(k0, tk), :])
        for head in range(2):
            vt_ref[ki, head, :e] = vt[head * e:(head + 1) * e]
            vt_ref[ki, head, e:] = jnp.ones((SUM_ROWS, tk), vt_ref.dtype)

    groups = tuple((head, half) for head in range(2) for half in range(halves))

    def q_body(qi, carry):
        q0 = pl.multiple_of(qi * tq, tq)
        for head in range(2):
            wq_ref[head] = _transpose_to_bf16(
                q_ref[pl.ds(q0, tq), head * LANES:(head + 1) * LANES])

        def finish_half(half):
            parts = []
            for head in range(2):
                a = acc_ref[head * halves + half]
                parts.append(a[:e] * (1.0 / a[e:e + 1]))
            r0 = pl.multiple_of(q0 + half * ATTN_G, ATTN_G)
            o_ref[pl.ds(r0, ATTN_G), :] = jnp.concatenate(parts, axis=0).astype(o_ref.dtype).T

        _attn_q_tile(
            qi, groups, mask_ref, acc_ref,
            k_tile=lambda ki, head: k_ref[pl.ds(pl.multiple_of(ki * tk, tk), tk),
                                          head * LANES:(head + 1) * LANES],
            wq_group=lambda head, half: wq_ref[head, :, half * ATTN_G:(half + 1) * ATTN_G],
            vt_tile=lambda ki, head: vt_ref[ki, head],
            fill_vt=fill_vt, finish_half=finish_half)
        return carry

    lax.fori_loop(0, q_ref.shape[0] // tq, q_body, 0)


def _mla_attention(q, k, v):
    b, s, _ = q.shape
    tq, tk = ATTN_TQ, ATTN_TK
    pairs = MLA_HEADS // 2
    return pl.pallas_call(
        _mla_attn_kernel,
        grid=(b, pairs),
        in_specs=[
            pl.BlockSpec((None, s, 2 * LANES), lambda bi, hp: (bi, 0, hp)),
            pl.BlockSpec((None, s, 2 * LANES), lambda bi, hp: (bi, 0, hp)),
            pl.BlockSpec((None, s, LANES), lambda bi, hp: (bi, 0, hp)),
        ],
        out_specs=pl.BlockSpec((None, s, LANES), lambda bi, hp: (bi, 0, hp)),
        out_shape=jax.ShapeDtypeStruct((b, s, MLA_HEADS * MLA_V), jnp.bfloat16),
        scratch_shapes=[
            pltpu.VMEM((s // tk, 2, MLA_V + SUM_ROWS, tk), jnp.bfloat16),
            pltpu.VMEM((2, LANES, tq), jnp.bfloat16),
            pltpu.VMEM((tk, ATTN_G), jnp.float32),
            pltpu.VMEM((2 * tq // ATTN_G, MLA_V + SUM_ROWS, ATTN_G), jnp.float32),
        ],
        compiler_params=_compiler_params(("parallel", "parallel")),
        name="mla_attn",
    )(q, k, v)


def _ffn_kernel(xh_ref, x_ref, oh_ref, o_ref, wo_ref, g_ref, wup_ref, cw_ref, wdn_ref, gf_ref,
                y_ref, h_ref, u_ref, acc_ref, *, pos_blocks, final_norm):
    i = pl.program_id(0)
    g = g_ref[...]
    o_all = jnp.concatenate([oh_ref[...], o_ref[...]], axis=0)
    x_all = jnp.concatenate([xh_ref[...], x_ref[...]], axis=0) + _dot(o_all, wo_ref[...])
    x = x_all[HALO:]
    h_all = _rms(x_all, g)
    h_ref[:HALO] = jnp.where(i % pos_blocks != 0, h_all[:HALO], 0.0).astype(h_ref.dtype)
    h_ref[HALO:] = h_all[HALO:].astype(h_ref.dtype)
    acc_ref[...] = jnp.zeros(acc_ref.shape, jnp.float32)
    tm = x.shape[0]
    nf = wdn_ref.shape[0]

    def up(f, slot):
        h = h_ref[...]
        for part in range(2):
            col = part * FFN_HIDDEN + f * FFN_CHUNK
            if not isinstance(col, int):
                col = pl.multiple_of(col, LANES)
            u_ref[slot, :, part * FFN_CHUNK:(part + 1) * FFN_CHUNK] = _dot(
                h, wup_ref[:, pl.ds(col, FFN_CHUNK)])

    def act(f, slot):
        u = u_ref.at[slot]
        cw = cw_ref[f]
        c = (u[pl.ds(HALO, tm), :] * cw[2:3] + u[pl.ds(HALO - 1, tm), :] * cw[1:2]
             + u[pl.ds(HALO - 2, tm), :] * cw[0:1] + cw[3:4])
        gate, val = c[:, :FFN_CHUNK], c[:, FFN_CHUNK:]
        return (gate / (1.0 + jnp.exp(-gate)) * val).astype(jnp.bfloat16)

    ring = u_ref.shape[0]
    assert ring >= FFN_LOOKAHEAD + 1

    def block(f0, n_chunks, n_prefetch):
        for k in range(1, min(FFN_LOOKAHEAD, n_prefetch) + 1):
            up(f0 + k, k % ring)
        for k in range(n_chunks):
            a = act(f0 + k, k % ring)
            acc_ref[...] += _dot(a, wdn_ref[f0 + k])
            ahead = k + FFN_LOOKAHEAD + 1
            if ahead <= n_prefetch:
                up(f0 + ahead, ahead % ring)

    def ring_walk(j, carry):
        block(ring * j, ring, ring)
        return carry

    up(0, 0)
    looped = (nf - 1) // ring * ring
    lax.fori_loop(0, looped // ring, ring_walk, 0)
    block(looped, nf - looped, nf - looped - 1)
    y = x + acc_ref[...]
    if final_norm:
        y = _rms(y, gf_ref[...])
    y_ref[...] = y


def _ffn(x2d, o2d, w_o, g, w_up, cw, w_down, g_final, seq_len, final_norm):
    t = x2d.shape[0]
    tm = ROW_TILE
    pos_blocks = seq_len // tm
    halo_per_tile = tm // HALO

    def whole(arr):
        nd = arr.ndim
        return pl.BlockSpec(arr.shape, lambda i: (0,) * nd, pipeline_mode=pl.Buffered(1))

    halo_spec = pl.BlockSpec((HALO, D_MODEL),
                             lambda i: (jnp.maximum(i * halo_per_tile - 1, 0), 0))
    main_spec = pl.BlockSpec((tm, D_MODEL), lambda i: (i, 0))
    return pl.pallas_call(
        functools.partial(_ffn_kernel, pos_blocks=pos_blocks, final_norm=final_norm),
        grid=(t // tm,),
        in_specs=[
            halo_spec, main_spec, halo_spec, main_spec, whole(w_o),
            whole(g), whole(w_up), whole(cw), whole(w_down), whole(g_final),
        ],
        out_specs=pl.BlockSpec((tm, D_MODEL), lambda i: (i, 0)),
        out_shape=jax.ShapeDtypeStruct((t, D_MODEL), jnp.float32),
        scratch_shapes=[
            pltpu.VMEM((HALO + tm, D_MODEL), jnp.bfloat16),
            pltpu.VMEM((FFN_RING, HALO + tm, 2 * FFN_CHUNK), jnp.float32),
            pltpu.VMEM((tm, D_MODEL), jnp.float32),
        ],
        compiler_params=_compiler_params(("parallel",)),
        name="conv_ffn",
    )(x2d, x2d, o2d, o2d, w_o, g, w_up, cw, w_down, g_final)


def _ffn_weights(w_up, conv_w, conv_b, w_down):
    nf = FFN_HIDDEN // FFN_CHUNK

    def chunks(a):
        lead = a.shape[:-1]
        a = a.reshape(lead + (2, nf, FFN_CHUNK))
        a = jnp.moveaxis(a, -2, 0)
        return a.reshape((nf,) + lead + (2 * FFN_CHUNK,))

    w_up_c = w_up.astype(jnp.bfloat16)
    taps = chunks(conv_w)
    bias = chunks(conv_b[None])
    cw = jnp.concatenate([taps, bias, jnp.zeros((nf, 4, 2 * FFN_CHUNK), jnp.float32)], axis=1)
    w_dn_c = w_down.reshape(nf, FFN_CHUNK, D_MODEL).astype(jnp.bfloat16)
    return w_up_c, cw, w_dn_c


def kernel(x, attn_norm_g, ffn_norm_g, da_w_qkv, da_lam_q1, da_lam_k1, da_lam_q2, da_lam_k2,
           da_subln_g, da_w_o, mla_w_in, mla_q_norm_g, mla_kv_norm_g, mla_w_uq, mla_w_ukv,
           mla_w_o, ffn_w_up, ffn_conv_w, ffn_conv_b, ffn_w_down, final_norm_g):
    b, s, d = x.shape
    assert d == D_MODEL and s % ROW_TILE == 0 and s % ATTN_TQ == 0 and ATTN_TQ % ATTN_TK == 0
    t = b * s
    bf16 = jnp.bfloat16
    x2d = x.reshape(t, d)
    g_final = final_norm_g.reshape(1, d)

    da_tables = _rope_tables(s, DA_ROT, 0, DA_HEAD_DIM)
    mla_k_tables = _mla_rope_tables(s)
    mla_scale = (MLA_NOPE + MLA_ROPE) ** -0.5 * LOG2_E
    mla_q_tables = tuple(tb * mla_scale for tb in mla_k_tables)

    for i in range(DEPTH):
        j = i // N_MIXERS
        g_attn = attn_norm_g[i].reshape(1, d)
        if i % N_MIXERS == 0:
            lambda_init = 0.8 - 0.6 * float(np.exp(-0.3 * i))
            qkv = _da_proj(x2d, g_attn, da_w_qkv[j].astype(bf16), da_tables, s)
            o = _da_attention(qkv.reshape(b, s, 3 * d),
                              da_lam_q1[j].reshape(1, -1), da_lam_k1[j].reshape(1, -1),
                              da_lam_q2[j].reshape(1, -1), da_lam_k2[j].reshape(1, -1),
                              jnp.broadcast_to(da_subln_g[j][:, None], (2 * DA_HEAD_DIM, ATTN_G)),
                              lambda_init)
            w_o = da_w_o[j]
        else:
            nope_src, rope_src = _mla_lane_map()
            qk_src = np.where(nope_src >= 0, nope_src,
                              np.where(rope_src >= 0, MLA_NOPE + rope_src, -1))
            pe_lo = MLA_Q_RANK + MLA_KV_RANK
            w_in = jnp.concatenate([mla_w_in[j][:, :pe_lo],
                                    _place_columns(mla_w_in[j][:, pe_lo:], rope_src)],
                                   axis=1).astype(bf16)
            w_uq = mla_w_uq[j].reshape(MLA_Q_RANK, MLA_HEADS, MLA_NOPE + MLA_ROPE)
            w_uq = _place_columns(w_uq, qk_src)
            w_uq = w_uq.reshape(MLA_Q_RANK, MLA_HEADS * LANES).astype(bf16)
            w_ukv = mla_w_ukv[j].reshape(MLA_KV_RANK, MLA_HEADS, MLA_NOPE + MLA_V)
            w_uk = _place_columns(w_ukv[:, :, :MLA_NOPE], nope_src)
            w_uk = w_uk.reshape(MLA_KV_RANK, MLA_HEADS * LANES).astype(bf16)
            w_uv = w_ukv[:, :, MLA_NOPE:].reshape(MLA_KV_RANK, MLA_HEADS * MLA_V).astype(bf16)
            q, k, v = _mla_proj(x2d, g_attn, w_in, mla_q_norm_g[j].reshape(1, -1),
                                mla_kv_norm_g[j].reshape(1, -1), w_uq, w_uk, w_uv,
                                mla_q_tables, mla_k_tables, s)
            o = _mla_attention(q.reshape(b, s, -1), k.reshape(b, s, -1), v.reshape(b, s, -1))
            w_o = mla_w_o[j]
        w_up_c, cw, w_dn_c = _ffn_weights(ffn_w_up[i], ffn_conv_w[i], ffn_conv_b[i],
                                          ffn_w_down[i])
        x2d = _ffn(x2d, o.reshape(t, d), w_o.astype(bf16), ffn_norm_g[i].reshape(1, d),
                   w_up_c, cw, w_dn_c, g_final, s, final_norm=(i == DEPTH - 1))
    return x2d.reshape(b, s, d)
```

```python
import functools

import numpy as np
import jax
import jax.numpy as jnp
from jax import lax
from jax.experimental import pallas as pl
from jax.experimental.pallas import tpu as pltpu

D_MODEL = 1024
DEPTH = 2
CHUNK = 64
ROPE_THETA = 500000.0
NORM_EPS = 1e-6
N_MIXERS = 2

DA_HEADS = 8
DA_HEAD_DIM = D_MODEL // (2 * DA_HEADS)
DA_ROT = DA_HEAD_DIM // 4

MLA_HEADS = 16
MLA_Q_RANK = 384
MLA_KV_RANK = 256
MLA_NOPE = 64
MLA_ROPE = 32
MLA_V = 64

FFN_HIDDEN = 2816
CONV_WIDTH = 3

LANES = 128
BF16_SUBLANES = 16
MASK_VALUE = -1e30

ROW_TILE = 512
ATTN_TQ = 1024
ATTN_TILES_PER_ITER = 4
ATTN_TK = 256
ATTN_G = 256
SUM_ROWS = BF16_SUBLANES
ATTN_LOOKAHEAD = 8
CHUNK_SHIFT = CHUNK.bit_length() - 1
LOG2_E = 1.4426950408889634
FFN_CHUNK = 256
FFN_RING = 5
FFN_LOOKAHEAD = 4
HALO = BF16_SUBLANES

VMEM_LIMIT = 56 * 1024 * 1024


def _compiler_params(semantics):
    return pltpu.CompilerParams(dimension_semantics=semantics,
                                vmem_limit_bytes=VMEM_LIMIT)


def _rms(x, g):
    return x * lax.rsqrt(jnp.mean(x * x, axis=-1, keepdims=True) + NORM_EPS) * g


def _dot(a, b):
    return jnp.dot(a, b, preferred_element_type=jnp.float32)


def _rope_block(x, c, a, bm, half):
    return (x * c + pltpu.roll(x, half, axis=1) * a
            + pltpu.roll(x, LANES - half, axis=1) * bm)


def _rope_tables(seq_len, rot_dim, lane_start, period):
    half = rot_dim // 2
    inv = ROPE_THETA ** (-jnp.arange(0, rot_dim, 2, dtype=jnp.float32) / rot_dim)
    ang = jnp.arange(seq_len, dtype=jnp.float32)[:, None] * inv[None, :]
    cos, sin = jnp.cos(ang), jnp.sin(ang)
    r = np.arange(LANES) % period - lane_start
    first = (r >= 0) & (r < half)
    second = (r >= half) & (r < rot_dim)
    idx = np.where(first, r, np.where(second, r - half, 0))
    cos_l, sin_l = cos[:, idx], sin[:, idx]
    c = jnp.where(first | second, cos_l, 1.0)
    a = jnp.where(second, sin_l, 0.0)
    bm = jnp.where(first, -sin_l, 0.0)
    return c, a, bm


def _mla_lane_map():
    half = MLA_ROPE // 2
    nope = np.full(LANES, -1)
    rope = np.full(LANES, -1)
    rope[:half] = half + np.arange(half)
    nope[half:LANES // 2] = np.arange(LANES // 2 - half)
    rope[LANES // 2:LANES // 2 + half] = np.arange(half)
    rest = MLA_NOPE - (LANES // 2 - half)
    nope[LANES // 2 + half:LANES // 2 + half + rest] = LANES // 2 - half + np.arange(rest)
    return nope, rope


def _place_columns(w, src):
    return jnp.where(src >= 0, w[..., np.maximum(src, 0)], 0.0)


def _mla_rope_tables(seq_len):
    half = MLA_ROPE // 2
    inv = ROPE_THETA ** (-jnp.arange(0, MLA_ROPE, 2, dtype=jnp.float32) / MLA_ROPE)
    ang = jnp.arange(seq_len, dtype=jnp.float32)[:, None] * inv[None, :]
    cos, sin = jnp.cos(ang), jnp.sin(ang)
    _, rope = _mla_lane_map()
    first, second = (rope >= 0) & (rope < half), rope >= half
    idx = np.where(rope >= 0, rope % half, 0)
    cos_l, sin_l = cos[:, idx], sin[:, idx]
    c = jnp.where(first | second, cos_l, 1.0)
    sg = jnp.where(first, -sin_l, jnp.where(second, sin_l, 0.0))
    return c, sg


def _rope_block_paired(x, c, sg):
    return x * c + pltpu.roll(x, LANES // 2, axis=1) * sg


def _da_proj_kernel(x_ref, g_ref, w_ref, c_ref, a_ref, bm_ref, o_ref):
    h = _rms(x_ref[...], g_ref[...]).astype(jnp.bfloat16)
    c, a, bm = c_ref[...], a_ref[...], bm_ref[...]
    scale = DA_HEAD_DIM ** -0.5 * LOG2_E
    for part in range(3):
        lo = part * D_MODEL
        y = _dot(h, w_ref[:, lo:lo + D_MODEL])
        for blk in range(D_MODEL // LANES):
            yb = y[:, blk * LANES:(blk + 1) * LANES]
            if part < 2:
                yb = _rope_block(yb, c, a, bm, DA_ROT // 2)
            if part == 0:
                yb = yb * scale
            o_ref[:, lo + blk * LANES:lo + (blk + 1) * LANES] = yb.astype(o_ref.dtype)


def _da_proj(x2d, g, w_qkv, tables, seq_len):
    t = x2d.shape[0]
    tm = ROW_TILE
    pos_blocks = seq_len // tm
    tab_spec = pl.BlockSpec((tm, LANES), lambda i: (i % pos_blocks, 0))
    return pl.pallas_call(
        _da_proj_kernel,
        grid=(t // tm,),
        in_specs=[
            pl.BlockSpec((tm, D_MODEL), lambda i: (i, 0)),
            pl.BlockSpec((1, D_MODEL), lambda i: (0, 0)),
            pl.BlockSpec((D_MODEL, 3 * D_MODEL), lambda i: (0, 0)),
            tab_spec, tab_spec, tab_spec,
        ],
        out_specs=pl.BlockSpec((tm, 3 * D_MODEL), lambda i: (i, 0)),
        out_shape=jax.ShapeDtypeStruct((t, 3 * D_MODEL), jnp.bfloat16),
        compiler_params=_compiler_params(("parallel",)),
        name="da_proj",
    )(x2d, g, w_qkv, *tables)


def _softmax_pv(s, vt1, m_prev, acc_ref):
    m_new = jnp.maximum(m_prev, jnp.max(s, axis=0, keepdims=True))
    alpha = jnp.exp2(m_prev - m_new)
    p = jnp.exp2(s - m_new).astype(vt1.dtype)
    acc_ref[...] = alpha * acc_ref[...] + _dot(vt1, p)
    return m_new


def _attn_q_tile(qi, groups, mask_ref, acc_ref, k_tile, wq_group, vt_tile, fill_vt,
                 finish_half):
    halves = ATTN_TQ // ATTN_G
    assert ATTN_TK == ATTN_G and halves % ATTN_TILES_PER_ITER == 0

    def scores(ki, slot, masked):
        stream, half = groups[slot]
        s = _dot(k_tile(ki, stream), wq_group(stream, half))
        if masked:
            s = jnp.where(mask_ref[...] > 0.0, s, MASK_VALUE)
        return s

    def run(items, ms, after=None):
        ms = list(ms)
        pending = [scores(*item) for item in items[:ATTN_LOOKAHEAD]]
        for n, (ki, slot, _) in enumerate(items):
            s = pending.pop(0)
            if n + ATTN_LOOKAHEAD < len(items):
                pending.append(scores(*items[n + ATTN_LOOKAHEAD]))
            ms[slot] = _softmax_pv(s, vt_tile(ki, groups[slot][0]), ms[slot], acc_ref.at[slot])
            if after and n in after:
                after[n]()
        return tuple(ms)

    def full_items(j):
        return [(ATTN_TILES_PER_ITER * j + t, slot, False)
                for t in range(ATTN_TILES_PER_ITER) for slot in range(len(groups))]

    def diagonal(ms, lead):
        items, after = list(lead), {}
        for d in range(halves):
            fill_vt(halves * qi + d)
            items += [(halves * qi + d, slot, half == d)
                      for slot, (_, half) in enumerate(groups) if half >= d]
            after[len(items) - 1] = functools.partial(finish_half, d)
        run(items, ms, after)

    acc_ref[...] = jnp.zeros(acc_ref.shape, jnp.float32)
    ms = tuple(jnp.full((1, ATTN_G), MASK_VALUE, jnp.float32) for _ in groups)
    trips = qi * (halves // ATTN_TILES_PER_ITER)
    ms = lax.fori_loop(0, jnp.maximum(trips - 1, 0), lambda j, c: run(full_items(j), c), ms)

    @pl.when(trips == 0)
    def _():
        diagonal(ms, [])

    @pl.when(trips > 0)
    def _():
        diagonal(ms, full_items(trips - 1))


def _fill_tri_mask(mask_ref):
    kc = lax.broadcasted_iota(jnp.int32, mask_ref.shape, 0) >> CHUNK_SHIFT
    qc = lax.broadcasted_iota(jnp.int32, mask_ref.shape, 1) >> CHUNK_SHIFT
    mask_ref[...] = (kc <= qc).astype(mask_ref.dtype)


def _transpose_to_bf16(x):
    return x.T


def _da_attn_kernel(lq1_ref, lk1_ref, lq2_ref, lk2_ref, sg_ref, q_ref, k_ref, v_ref,
                    o_ref, vt_ref, wq_ref, mask_ref, acc_ref, *, lambda_init):
    tq, tk, e = ATTN_TQ, ATTN_TK, 2 * DA_HEAD_DIM
    halves = tq // ATTN_G
    _fill_tri_mask(mask_ref)

    def fill_vt(ki):
        k0 = pl.multiple_of(ki * tk, tk)
        vt_ref[ki, :e] = _transpose_to_bf16(v_ref[pl.ds(k0, tk), :])
        vt_ref[ki, e:] = jnp.ones((SUM_ROWS, tk), vt_ref.dtype)

    lam = (jnp.exp(jnp.sum(lq1_ref[...] * lk1_ref[...], keepdims=True))
           - jnp.exp(jnp.sum(lq2_ref[...] * lk2_ref[...], keepdims=True))
           + lambda_init)
    groups = tuple((comp, half) for comp in range(2) for half in range(halves))

    def q_body(qi, carry):
        q0 = pl.multiple_of(qi * tq, tq)
        qt = q_ref[pl.ds(q0, tq), :].T
        row = lax.broadcasted_iota(jnp.int32, qt.shape, 0)
        zero = jnp.zeros_like(qt)
        wq_ref[0] = jnp.where(row < DA_HEAD_DIM, qt, zero)
        wq_ref[1] = jnp.where(row >= DA_HEAD_DIM, qt, zero)

        def finish_half(half):
            a0, a1 = acc_ref[half], acc_ref[halves + half]
            o = a0[:e] * (1.0 / a0[e:e + 1]) - lam * (a1[:e] * (1.0 / a1[e:e + 1]))
            inv = lax.rsqrt(jnp.mean(o * o, axis=0, keepdims=True) + NORM_EPS)
            o = o * inv * sg_ref[...] * (1.0 - lambda_init)
            r0 = pl.multiple_of(q0 + half * ATTN_G, ATTN_G)
            o_ref[pl.ds(r0, ATTN_G), :] = o.astype(o_ref.dtype).T

        _attn_q_tile(
            qi, groups, mask_ref, acc_ref,
            k_tile=lambda ki, comp: k_ref[pl.ds(pl.multiple_of(ki * tk, tk), tk), :],
            wq_group=lambda comp, half: wq_ref[comp, :, half * ATTN_G:(half + 1) * ATTN_G],
            vt_tile=lambda ki, comp: vt_ref[ki], fill_vt=fill_vt, finish_half=finish_half)
        return carry

    lax.fori_loop(0, q_ref.shape[0] // tq, q_body, 0)


def _da_attention(qkv, lq1, lk1, lq2, lk2, subln_g, lambda_init):
    b, s, _ = qkv.shape
    tq, tk = ATTN_TQ, ATTN_TK
    hd = 2 * DA_HEAD_DIM
    vec = pl.BlockSpec((1, DA_HEAD_DIM), lambda bi, h: (0, 0))
    return pl.pallas_call(
        functools.partial(_da_attn_kernel, lambda_init=lambda_init),
        grid=(b, DA_HEADS),
        in_specs=[
            vec, vec, vec, vec,
            pl.BlockSpec((hd, ATTN_G), lambda bi, h: (0, 0)),
            pl.BlockSpec((None, s, hd), lambda bi, h: (bi, 0, h)),
            pl.BlockSpec((None, s, hd), lambda bi, h: (bi, 0, DA_HEADS + h)),
            pl.BlockSpec((None, s, hd), lambda bi, h: (bi, 0, 2 * DA_HEADS + h)),
        ],
        out_specs=pl.BlockSpec((None, s, hd), lambda bi, h: (bi, 0, h)),
        out_shape=jax.ShapeDtypeStruct((b, s, DA_HEADS * hd), jnp.bfloat16),
        scratch_shapes=[
            pltpu.VMEM((s // tk, hd + SUM_ROWS, tk), jnp.bfloat16),
            pltpu.VMEM((2, hd, tq), jnp.bfloat16),
            pltpu.VMEM((tk, ATTN_G), jnp.float32),
            pltpu.VMEM((2 * tq // ATTN_G, hd + SUM_ROWS, ATTN_G), jnp.float32),
        ],
        compiler_params=_compiler_params(("parallel", "parallel")),
        name="da_attn",
    )(lq1, lk1, lq2, lk2, subln_g, qkv, qkv, qkv)


def _mla_proj_kernel(x_ref, g_ref, win_ref, gq_ref, gkv_ref, wuq_ref, wuk_ref, wuv_ref,
                     cq_ref, sq_ref, ck_ref, sk_ref, q_ref, k_ref, v_ref):
    kv_lo = MLA_Q_RANK
    pe_lo = MLA_Q_RANK + MLA_KV_RANK
    tm = x_ref.shape[0]
    for rows in (pl.ds(0, tm // 2), pl.ds(tm // 2, tm // 2)):
        h = _rms(x_ref[rows, :], g_ref[...]).astype(jnp.bfloat16)
        proj = _dot(h, win_ref[...])
        c_q = _rms(proj[:, :kv_lo], gq_ref[...]).astype(jnp.bfloat16)
        c_kv = _rms(proj[:, kv_lo:pe_lo], gkv_ref[...]).astype(jnp.bfloat16)
        k_pe = _rope_block_paired(proj[:, pe_lo:pe_lo + LANES], ck_ref[rows, :], sk_ref[rows, :])
        cq, sq = cq_ref[rows, :], sq_ref[rows, :]
        q = _dot(c_q, wuq_ref[...])
        k = _dot(c_kv, wuk_ref[...])
        for hd in range(MLA_HEADS):
            sl = slice(hd * LANES, (hd + 1) * LANES)
            q_ref[rows, sl] = _rope_block_paired(q[:, sl], cq, sq).astype(q_ref.dtype)
            k_ref[rows, sl] = (k[:, sl] + k_pe).astype(k_ref.dtype)
        v_ref[rows, :] = _dot(c_kv, wuv_ref[...]).astype(v_ref.dtype)


def _mla_proj(x2d, g, w_in, g_q, g_kv, w_uq, w_uk, w_uv, q_tables, k_tables, seq_len):
    t = x2d.shape[0]
    tm = ROW_TILE
    pos_blocks = seq_len // tm
    tab_spec = pl.BlockSpec((tm, LANES), lambda i: (i % pos_blocks, 0))

    def whole(arr):
        return pl.BlockSpec(arr.shape, lambda i: (0, 0))

    qk_cols = MLA_HEADS * LANES
    v_cols = MLA_HEADS * MLA_V
    return pl.pallas_call(
        _mla_proj_kernel,
        grid=(t // tm,),
        in_specs=[pl.BlockSpec((tm, D_MODEL), lambda i: (i, 0)),
                  whole(g), whole(w_in), whole(g_q), whole(g_kv),
                  whole(w_uq), whole(w_uk), whole(w_uv)] + [tab_spec] * 4,
        out_specs=[pl.BlockSpec((tm, qk_cols), lambda i: (i, 0)),
                   pl.BlockSpec((tm, qk_cols), lambda i: (i, 0)),
                   pl.BlockSpec((tm, v_cols), lambda i: (i, 0))],
        out_shape=[jax.ShapeDtypeStruct((t, qk_cols), jnp.bfloat16),
                   jax.ShapeDtypeStruct((t, qk_cols), jnp.bfloat16),
                   jax.ShapeDtypeStruct((t, v_cols), jnp.bfloat16)],
        compiler_params=_compiler_params(("parallel",)),
        name="mla_proj",
    )(x2d, g, w_in, g_q, g_kv, w_uq, w_uk, w_uv, *q_tables, *k_tables)


def _mla_attn_kernel(q_ref, k_ref, v_ref, o_ref, vt_ref, wq_ref, mask_ref, acc_ref):
    tq, tk, e = ATTN_TQ, ATTN_TK, MLA_V
    halves = tq // ATTN_G
    _fill_tri_mask(mask_ref)

    def fill_vt(ki):
        k0 = pl.multiple_of(ki * tk, tk)
        vt = _transpose_to_bf16(v_ref[pl.ds(k0, tk), :])
        for head in range(2):
            vt_ref[ki, head, :e] = vt[head * e:(head + 1) * e]
            vt_ref[ki, head, e:] = jnp.ones((SUM_ROWS, tk), vt_ref.dtype)

    groups = tuple((head, half) for head in range(2) for half in range(halves))

    def q_body(qi, carry):
        q0 = pl.multiple_of(qi * tq, tq)
        for head in range(2):
            wq_ref[head] = _transpose_to_bf16(
                q_ref[pl.ds(q0, tq), head * LANES:(head + 1) * LANES])

        def finish_half(half):
            parts = []
            for head in range(2):
                a = acc_ref[head * halves + half]
                parts.append(a[:e] * (1.0 / a[e:e + 1]))
            r0 = pl.multiple_of(q0 + half * ATTN_G, ATTN_G)
            o_ref[pl.ds(r0, ATTN_G), :] = jnp.concatenate(parts, axis=0).astype(o_ref.dtype).T

        _attn_q_tile(
            qi, groups, mask_ref, acc_ref,
            k_tile=lambda ki, head: k_ref[pl.ds(pl.multiple_of(ki * tk, tk), tk),
                                          head * LANES:(head + 1) * LANES],
            wq_group=lambda head, half: wq_ref[head, :, half * ATTN_G:(half + 1) * ATTN_G],
            vt_tile=lambda ki, head: vt_ref[ki, head],
            fill_vt=fill_vt, finish_half=finish_half)
        return carry

    lax.fori_loop(0, q_ref.shape[0] // tq, q_body, 0)


def _mla_attention(q, k, v):
    b, s, _ = q.shape
    tq, tk = ATTN_TQ, ATTN_TK
    pairs = MLA_HEADS // 2
    return pl.pallas_call(
        _mla_attn_kernel,
        grid=(b, pairs),
        in_specs=[
            pl.BlockSpec((None, s, 2 * LANES), lambda bi, hp: (bi, 0, hp)),
            pl.BlockSpec((None, s, 2 * LANES), lambda bi, hp: (bi, 0, hp)),
            pl.BlockSpec((None, s, LANES), lambda bi, hp: (bi, 0, hp)),
        ],
        out_specs=pl.BlockSpec((None, s, LANES), lambda bi, hp: (bi, 0, hp)),
        out_shape=jax.ShapeDtypeStruct((b, s, MLA_HEADS * MLA_V), jnp.bfloat16),
        scratch_shapes=[
            pltpu.VMEM((s // tk, 2, MLA_V + SUM_ROWS, tk), jnp.bfloat16),
            pltpu.VMEM((2, LANES, tq), jnp.bfloat16),
            pltpu.VMEM((tk, ATTN_G), jnp.float32),
            pltpu.VMEM((2 * tq // ATTN_G, MLA_V + SUM_ROWS, ATTN_G), jnp.float32),
        ],
        compiler_params=_compiler_params(("parallel", "parallel")),
        name="mla_attn",
    )(q, k, v)


def _ffn_kernel(xh_ref, x_ref, oh_ref, o_ref, wo_ref, g_ref, wup_ref, cw_ref, wdn_ref, gf_ref,
                y_ref, h_ref, u_ref, acc_ref, *, pos_blocks, final_norm):
    i = pl.program_id(0)
    g = g_ref[...]
    o_all = jnp.concatenate([oh_ref[...], o_ref[...]], axis=0)
    x_all = jnp.concatenate([xh_ref[...], x_ref[...]], axis=0) + _dot(o_all, wo_ref[...])
    x = x_all[HALO:]
    h_all = _rms(x_all, g)
    h_ref[:HALO] = jnp.where(i % pos_blocks != 0, h_all[:HALO], 0.0).astype(h_ref.dtype)
    h_ref[HALO:] = h_all[HALO:].astype(h_ref.dtype)
    acc_ref[...] = jnp.zeros(acc_ref.shape, jnp.float32)
    tm = x.shape[0]
    nf = wdn_ref.shape[0]

    def up(f, slot):
        h = h_ref[...]
        for part in range(2):
            col = part * FFN_HIDDEN + f * FFN_CHUNK
            if not isinstance(col, int):
                col = pl.multiple_of(col, LANES)
            u_ref[slot, :, part * FFN_CHUNK:(part + 1) * FFN_CHUNK] = _dot(
                h, wup_ref[:, pl.ds(col, FFN_CHUNK)])

    def act(f, slot):
        u = u_ref.at[slot]
        cw = cw_ref[f]
        c = (u[pl.ds(HALO, tm), :] * cw[2:3] + u[pl.ds(HALO - 1, tm), :] * cw[1:2]
             + u[pl.ds(HALO - 2, tm), :] * cw[0:1] + cw[3:4])
        gate, val = c[:, :FFN_CHUNK], c[:, FFN_CHUNK:]
        return (gate / (1.0 + jnp.exp(-gate)) * val).astype(jnp.bfloat16)

    ring = u_ref.shape[0]
    assert ring >= FFN_LOOKAHEAD + 1

    def block(f0, n_chunks, n_prefetch):
        for k in range(1, min(FFN_LOOKAHEAD, n_prefetch) + 1):
            up(f0 + k, k % ring)
        for k in range(n_chunks):
            a = act(f0 + k, k % ring)
            acc_ref[...] += _dot(a, wdn_ref[f0 + k])
            ahead = k + FFN_LOOKAHEAD + 1
            if ahead <= n_prefetch:
                up(f0 + ahead, ahead % ring)

    def ring_walk(j, carry):
        block(ring * j, ring, ring)
        return carry

    up(0, 0)
    looped = (nf - 1) // ring * ring
    lax.fori_loop(0, looped // ring, ring_walk, 0)
    block(looped, nf - looped, nf - looped - 1)
    y = x + acc_ref[...]
    if final_norm:
        y = _rms(y, gf_ref[...])
    y_ref[...] = y


def _ffn(x2d, o2d, w_o, g, w_up, cw, w_down, g_final, seq_len, final_norm):
    t = x2d.shape[0]
    tm = ROW_TILE
    pos_blocks = seq_len // tm
    halo_per_tile = tm // HALO

    def whole(arr):
        nd = arr.ndim
        return pl.BlockSpec(arr.shape, lambda i: (0,) * nd, pipeline_mode=pl.Buffered(1))

    halo_spec = pl.BlockSpec((HALO, D_MODEL),
                             lambda i: (jnp.maximum(i * halo_per_tile - 1, 0), 0))
    main_spec = pl.BlockSpec((tm, D_MODEL), lambda i: (i, 0))
    return pl.pallas_call(
        functools.partial(_ffn_kernel, pos_blocks=pos_blocks, final_norm=final_norm),
        grid=(t // tm,),
        in_specs=[
            halo_spec, main_spec, halo_spec, main_spec, whole(w_o),
            whole(g), whole(w_up), whole(cw), whole(w_down), whole(g_final),
        ],
        out_specs=pl.BlockSpec((tm, D_MODEL), lambda i: (i, 0)),
        out_shape=jax.ShapeDtypeStruct((t, D_MODEL), jnp.float32),
        scratch_shapes=[
            pltpu.VMEM((HALO + tm, D_MODEL), jnp.bfloat16),
            pltpu.VMEM((FFN_RING, HALO + tm, 2 * FFN_CHUNK), jnp.float32),
            pltpu.VMEM((tm, D_MODEL), jnp.float32),
        ],
        compiler_params=_compiler_params(("parallel",)),
        name="conv_ffn",
    )(x2d, x2d, o2d, o2d, w_o, g, w_up, cw, w_down, g_final)


def _ffn_weights(w_up, conv_w, conv_b, w_down):
    nf = FFN_HIDDEN // FFN_CHUNK

    def chunks(a):
        lead = a.shape[:-1]
        a = a.reshape(lead + (2, nf, FFN_CHUNK))
        a = jnp.moveaxis(a, -2, 0)
        return a.reshape((nf,) + lead + (2 * FFN_CHUNK,))

    w_up_c = w_up.astype(jnp.bfloat16)
    taps = chunks(conv_w)
    bias = chunks(conv_b[None])
    cw = jnp.concatenate([taps, bias, jnp.zeros((nf, 4, 2 * FFN_CHUNK), jnp.float32)], axis=1)
    w_dn_c = w_down.reshape(nf, FFN_CHUNK, D_MODEL).astype(jnp.bfloat16)
    return w_up_c, cw, w_dn_c


def kernel(x, attn_norm_g, ffn_norm_g, da_w_qkv, da_lam_q1, da_lam_k1, da_lam_q2, da_lam_k2,
           da_subln_g, da_w_o, mla_w_in, mla_q_norm_g, mla_kv_norm_g, mla_w_uq, mla_w_ukv,
           mla_w_o, ffn_w_up, ffn_conv_w, ffn_conv_b, ffn_w_down, final_norm_g):
    b, s, d = x.shape
    assert d == D_MODEL and s % ROW_TILE == 0 and s % ATTN_TQ == 0 and ATTN_TQ % ATTN_TK == 0
    t = b * s
    bf16 = jnp.bfloat16
    x2d = x.reshape(t, d)
    g_final = final_norm_g.reshape(1, d)

    da_tables = _rope_tables(s, DA_ROT, 0, DA_HEAD_DIM)
    mla_k_tables = _mla_rope_tables(s)
    mla_scale = (MLA_NOPE + MLA_ROPE) ** -0.5 * LOG2_E
    mla_q_tables = tuple(tb * mla_scale for tb in mla_k_tables)

    for i in range(DEPTH):
        j = i // N_MIXERS
        g_attn = attn_norm_g[i].reshape(1, d)
        if i % N_MIXERS == 0:
            lambda_init = 0.8 - 0.6 * float(np.exp(-0.3 * i))
            qkv = _da_proj(x2d, g_attn, da_w_qkv[j].astype(bf16), da_tables, s)
            o = _da_attention(qkv.reshape(b, s, 3 * d),
                              da_lam_q1[j].reshape(1, -1), da_lam_k1[j].reshape(1, -1),
                              da_lam_q2[j].reshape(1, -1), da_lam_k2[j].reshape(1, -1),
                              jnp.broadcast_to(da_subln_g[j][:, None], (2 * DA_HEAD_DIM, ATTN_G)),
                              lambda_init)
            w_o = da_w_o[j]
        else:
            nope_src, rope_src = _mla_lane_map()
            qk_src = np.where(nope_src >= 0, nope_src,
                              np.where(rope_src >= 0, MLA_NOPE + rope_src, -1))
            pe_lo = MLA_Q_RANK + MLA_KV_RANK
            w_in = jnp.concatenate([mla_w_in[j][:, :pe_lo],
                                    _place_columns(mla_w_in[j][:, pe_lo:], rope_src)],
                                   axis=1).astype(bf16)
            w_uq = mla_w_uq[j].reshape(MLA_Q_RANK, MLA_HEADS, MLA_NOPE + MLA_ROPE)
            w_uq = _place_columns(w_uq, qk_src)
            w_uq = w_uq.reshape(MLA_Q_RANK, MLA_HEADS * LANES).astype(bf16)
            w_ukv = mla_w_ukv[j].reshape(MLA_KV_RANK, MLA_HEADS, MLA_NOPE + MLA_V)
            w_uk = _place_columns(w_ukv[:, :, :MLA_NOPE], nope_src)
            w_uk = w_uk.reshape(MLA_KV_RANK, MLA_HEADS * LANES).astype(bf16)
            w_uv = w_ukv[:, :, MLA_NOPE:].reshape(MLA_KV_RANK, MLA_HEADS * MLA_V).astype(bf16)
            q, k, v = _mla_proj(x2d, g_attn, w_in, mla_q_norm_g[j].reshape(1, -1),
                                mla_kv_norm_g[j].reshape(1, -1), w_uq, w_uk, w_uv,
                                mla_q_tables, mla_k_tables, s)
            o = _mla_attention(q.reshape(b, s, -1), k.reshape(b, s, -1), v.reshape(b, s, -1))
            w_o = mla_w_o[j]
        w_up_c, cw, w_dn_c = _ffn_weights(ffn_w_up[i], ffn_conv_w[i], ffn_conv_b[i],
                                          ffn_w_down[i])
        x2d = _ffn(x2d, o.reshape(t, d), w_o.astype(bf16), ffn_norm_g[i].reshape(1, d),
                   w_up_c, cw, w_dn_c, g_final, s, final_norm=(i == DEPTH - 1))
    return x2d.reshape(b, s, d)
```
